```python
import math
import jax, jax.numpy as jnp
from jax import lax
import numpy as np

D_MODEL = 2048
BATCH = 4
SEQ = 8192
DEPTH = 4

CHUNK = 64
N_MIXERS = 2
Q_BLOCK = 128

MLA_HEADS = 16
QK_NOPE = 128
QK_ROPE = 64
V_HEAD = 128
Q_LORA = 512
KV_LORA = 512
ROPE_THETA = 10000.0
MLA_QK = QK_NOPE + QK_ROPE
MLA_WIDTH = MLA_HEADS * V_HEAD
MLA_IN = Q_LORA + KV_LORA + QK_ROPE + MLA_WIDTH

ML_HEADS = 8
ML_QK = 128
ML_V = 256
ML_QK_WIDTH = ML_HEADS * ML_QK
ML_WIDTH = ML_HEADS * ML_V
CONV_W = 4
ML_IN = 2 * ML_QK_WIDTH + 2 * ML_WIDTH + 2 * ML_HEADS + ML_WIDTH

ALPHA = (2 * DEPTH) ** 0.25
BETA = (8 * DEPTH) ** -0.25
EPS = 1e-6

kernel_name = 'hybrid_mla_mlstm_deepnorm'


def _rmsnorm(x, w):
    xf = x.astype(jnp.float32)
    y = xf * lax.rsqrt(jnp.mean(xf * xf, axis=-1, keepdims=True) + EPS)
    return y.astype(x.dtype) * w


def _layernorm(x, w, b):
    xf = x.astype(jnp.float32)
    mu = jnp.mean(xf, axis=-1, keepdims=True)
    var = jnp.mean(jnp.square(xf - mu), axis=-1, keepdims=True)
    return ((xf - mu) * lax.rsqrt(var + EPS)).astype(x.dtype) * w + b


def _rope_tables(positions, dtype):
    inv_freq = ROPE_THETA ** (-jnp.arange(0, QK_ROPE, 2, dtype=jnp.float32) / QK_ROPE)
    ang = positions.astype(jnp.float32)[..., None] * inv_freq
    return jnp.cos(ang).astype(dtype), jnp.sin(ang).astype(dtype)


def _rope(x, cos, sin):
    half = x.shape[-1] // 2
    x1, x2 = x[..., :half], x[..., half:]
    return jnp.concatenate([x1 * cos - x2 * sin, x1 * sin + x2 * cos], axis=-1)


def _chunk_causal_attention(q, k, v):
    b, s, h, dk = q.shape
    nb = s // Q_BLOCK
    scale = dk ** -0.5
    q_blocks = q.reshape(b, nb, Q_BLOCK, h, dk).transpose(1, 0, 2, 3, 4)
    key_chunk = jnp.arange(s) // CHUNK

    def one_block(args):
        qb, bi = args
        q_chunk = (bi * Q_BLOCK + jnp.arange(Q_BLOCK)) // CHUNK
        mask = key_chunk[None, :] <= q_chunk[:, None]
        sc = jnp.einsum('bqhd,bkhd->bhqk', qb, k).astype(jnp.float32) * scale
        sc = jnp.where(mask, sc, -jnp.inf)
        p = jax.nn.softmax(sc, axis=-1).astype(v.dtype)
        return jnp.einsum('bhqk,bkhd->bqhd', p, v)

    out = lax.map(one_block, (q_blocks, jnp.arange(nb)))
    return out.transpose(1, 0, 2, 3, 4).reshape(b, s, h, v.shape[-1])


def _mla_mixer(x, cos, sin, w_in, q_norm, w_qb, kv_norm, w_kvb, w_out):
    b, s, _ = x.shape
    proj = x @ w_in
    c_q, c_kv, k_rope, z = jnp.split(
        proj, [Q_LORA, Q_LORA + KV_LORA, Q_LORA + KV_LORA + QK_ROPE], axis=-1)
    q = (_rmsnorm(c_q, q_norm) @ w_qb).reshape(b, s, MLA_HEADS, MLA_QK)
    q_nope, q_rope = q[..., :QK_NOPE], q[..., QK_NOPE:]
    kv = (_rmsnorm(c_kv, kv_norm) @ w_kvb).reshape(b, s, MLA_HEADS, QK_NOPE + V_HEAD)
    k_nope, v = kv[..., :QK_NOPE], kv[..., QK_NOPE:]
    q_rope = _rope(q_rope, cos[:, :, None, :], sin[:, :, None, :])
    k_rope = _rope(k_rope, cos, sin)
    q = jnp.concatenate([q_nope, q_rope], axis=-1)
    k = jnp.concatenate(
        [k_nope, jnp.broadcast_to(k_rope[:, :, None, :], (b, s, MLA_HEADS, QK_ROPE))], axis=-1)
    attn = _chunk_causal_attention(q, k, v).reshape(b, s, MLA_WIDTH)
    return (attn * jax.nn.silu(z)) @ w_out


def _causal_depthwise_conv(x, w, bias):
    c = x.shape[-1]
    y = lax.conv_general_dilated(
        x, w[:, None, :], window_strides=(1,), padding=[(CONV_W - 1, 0)],
        dimension_numbers=('NWC', 'WIO', 'NWC'), feature_group_count=c)
    return y + bias


def _mlstm_chunkwise(q, k, v, i_pre, f_pre):
    b, s, h, dk = q.shape
    dv = v.shape[-1]
    nc = s // CHUNK
    f32 = jnp.float32

    def chunks(t):
        return t.astype(f32).reshape(b, nc, CHUNK, h, -1).transpose(1, 0, 3, 2, 4)

    def gate_chunks(t):
        return t.astype(f32).reshape(b, nc, CHUNK, h).transpose(1, 0, 3, 2)

    qc = chunks(q)
    kc = chunks(k) * (dk ** -0.5)
    vc = chunks(v)
    log_i = gate_chunks(i_pre)
    log_f = jax.nn.log_sigmoid(gate_chunks(f_pre))
    tril = jnp.tril(jnp.ones((CHUNK, CHUNK), dtype=bool))

    def step(carry, xs):
        c_st, n_st, m_st = carry
        qx, kx, vx, li, lf = xs
        bcum = jnp.cumsum(lf, axis=-1)
        dmat = bcum[..., :, None] - bcum[..., None, :] + li[..., None, :]
        dmat = jnp.where(tril, dmat, -jnp.inf)
        m_inter = bcum + m_st[..., None]
        m_t = jnp.maximum(m_inter, jnp.max(dmat, axis=-1))
        w_inter = jnp.exp(m_inter - m_t)
        s_qk = jnp.einsum('bhtd,bhsd->bhts', qx, kx) * jnp.exp(dmat - m_t[..., None])
        num = (w_inter[..., None] * jnp.einsum('bhtd,bhdv->bhtv', qx, c_st)
               + jnp.einsum('bhts,bhsv->bhtv', s_qk, vx))
        den = w_inter * jnp.einsum('bhtd,bhd->bht', qx, n_st) + jnp.sum(s_qk, axis=-1)
        h_out = num / jnp.maximum(jnp.abs(den), jnp.exp(-m_t))[..., None]
        b_last = bcum[..., -1]
        g = b_last[..., None] - bcum + li
        m_new = jnp.maximum(b_last + m_st, jnp.max(g, axis=-1))
        decay = jnp.exp(b_last + m_st - m_new)
        wk = jnp.exp(g - m_new[..., None])
        c_new = decay[..., None, None] * c_st + jnp.einsum('bhs,bhsd,bhsv->bhdv', wk, kx, vx)
        n_new = decay[..., None] * n_st + jnp.einsum('bhs,bhsd->bhd', wk, kx)
        return (c_new, n_new, m_new), h_out

    init = (jnp.zeros((b, h, dk, dv), f32), jnp.zeros((b, h, dk), f32), jnp.zeros((b, h), f32))
    _, hs = lax.scan(step, init, (qc, kc, vc, log_i, log_f))
    return hs.transpose(1, 0, 3, 2, 4).reshape(b, s, h, dv)


def _mlstm_mixer(x, w_in, conv_w, conv_b, gate_b, head_norm, w_out):
    b, s, _ = x.shape
    proj = x @ w_in
    o1 = 2 * ML_QK_WIDTH
    o2 = o1 + ML_WIDTH
    o3 = o2 + ML_WIDTH
    o4 = o3 + 2 * ML_HEADS
    qk, v, o, gates, z = jnp.split(proj, [o1, o2, o3, o4], axis=-1)
    qk = jax.nn.silu(_causal_depthwise_conv(qk, conv_w, conv_b))
    q, k = jnp.split(qk, 2, axis=-1)
    gates = gates + gate_b
    i_pre, f_pre = gates[..., :ML_HEADS], gates[..., ML_HEADS:]
    cell = _mlstm_chunkwise(q.reshape(b, s, ML_HEADS, ML_QK), k.reshape(b, s, ML_HEADS, ML_QK),
                            v.reshape(b, s, ML_HEADS, ML_V), i_pre, f_pre)
    cell = _rmsnorm(cell.astype(x.dtype), head_norm.reshape(ML_HEADS, ML_V)).reshape(b, s, ML_WIDTH)
    hcell = jax.nn.sigmoid(o) * cell
    return (hcell * jax.nn.silu(z)) @ w_out


def setup_inputs(seed: int = 0) -> dict:
    key = jax.random.key(seed)
    ks = jax.random.split(key, 20)
    n_a = (DEPTH + N_MIXERS - 1) // N_MIXERS
    n_b = DEPTH // N_MIXERS

    def nrm(k, shape, scale):
        return jax.random.normal(k, shape, jnp.float32) * scale

    x = nrm(ks[0], (BATCH, SEQ, D_MODEL), 1.0)
    offsets = jax.random.randint(ks[1], (BATCH, 1), 0, 4096, dtype=jnp.int32)
    positions = offsets + jnp.arange(SEQ, dtype=jnp.int32)[None, :]

    mla_w_in = nrm(ks[2], (n_a, D_MODEL, MLA_IN), D_MODEL ** -0.5)
    mla_q_norm = 1.0 + nrm(ks[3], (n_a, Q_LORA), 0.02)
    mla_w_qb = nrm(ks[4], (n_a, Q_LORA, MLA_HEADS * MLA_QK), Q_LORA ** -0.5)
    mla_kv_norm = 1.0 + nrm(ks[5], (n_a, KV_LORA), 0.02)
    mla_w_kvb = nrm(ks[6], (n_a, KV_LORA, MLA_HEADS * (QK_NOPE + V_HEAD)), KV_LORA ** -0.5)
    mla_w_out = nrm(ks[7], (n_a, MLA_WIDTH, D_MODEL), BETA * MLA_WIDTH ** -0.5)

    ml_w_in = nrm(ks[8], (n_b, D_MODEL, ML_IN), D_MODEL ** -0.5)
    ml_conv_w = nrm(ks[9], (n_b, CONV_W, 2 * ML_QK_WIDTH), CONV_W ** -0.5)
    ml_conv_b = nrm(ks[10], (n_b, 2 * ML_QK_WIDTH), 0.02)
    i_bias = nrm(ks[11], (n_b, ML_HEADS), 0.1)
    f_bias = jnp.linspace(3.0, 6.0, ML_HEADS, dtype=jnp.float32)[None, :] + nrm(ks[12], (n_b, ML_HEADS), 0.1)
    ml_gate_b = jnp.concatenate([i_bias, f_bias], axis=-1)
    ml_head_norm = 1.0 + nrm(ks[13], (n_b, ML_WIDTH), 0.02)
    ml_w_out = nrm(ks[14], (n_b, ML_WIDTH, D_MODEL), BETA * ML_WIDTH ** -0.5)

    ln_w = 1.0 + nrm(ks[15], (DEPTH, D_MODEL), 0.02)
    ln_b = nrm(ks[16], (DEPTH, D_MODEL), 0.02)
    return {'x': x, 'positions': positions,
            'mla_w_in': mla_w_in, 'mla_q_norm': mla_q_norm, 'mla_w_qb': mla_w_qb,
            'mla_kv_norm': mla_kv_norm, 'mla_w_kvb': mla_w_kvb, 'mla_w_out': mla_w_out,
            'ml_w_in': ml_w_in, 'ml_conv_w': ml_conv_w, 'ml_conv_b': ml_conv_b,
            'ml_gate_b': ml_gate_b, 'ml_head_norm': ml_head_norm, 'ml_w_out': ml_w_out,
            'ln_w': ln_w, 'ln_b': ln_b}


def reference(x, positions, mla_w_in, mla_q_norm, mla_w_qb, mla_kv_norm, mla_w_kvb, mla_w_out,
              ml_w_in, ml_conv_w, ml_conv_b, ml_gate_b, ml_head_norm, ml_w_out, ln_w, ln_b):
    cos, sin = _rope_tables(positions, x.dtype)
    for layer in range(DEPTH):
        j = layer // N_MIXERS
        if layer % N_MIXERS == 0:
            y = _mla_mixer(x, cos, sin, mla_w_in[j], mla_q_norm[j], mla_w_qb[j],
                           mla_kv_norm[j], mla_w_kvb[j], mla_w_out[j])
        else:
            y = _mlstm_mixer(x, ml_w_in[j], ml_conv_w[j], ml_conv_b[j], ml_gate_b[j],
                             ml_head_norm[j], ml_w_out[j])
        x = _layernorm(ALPHA * x + y, ln_w[layer], ln_b[layer])
    return x
```

```python
import functools

import jax
import jax.numpy as jnp
from jax import lax
from jax.experimental import pallas as pl
from jax.experimental.pallas import tpu as pltpu

D_MODEL = 2048
DEPTH = 4
CHUNK = 64
N_MIXERS = 2

MLA_HEADS = 16
QK_NOPE = 128
QK_ROPE = 64
V_HEAD = 128
Q_LORA = 512
KV_LORA = 512
ROPE_THETA = 10000.0
MLA_QK = QK_NOPE + QK_ROPE
MLA_WIDTH = MLA_HEADS * V_HEAD
QK_PAD = 256

ML_HEADS = 8
ML_QK = 128
ML_V = 256
ML_QK_WIDTH = ML_HEADS * ML_QK
ML_WIDTH = ML_HEADS * ML_V
CONV_W = 4
GATE_PAD = 128

ALPHA = (2 * DEPTH) ** 0.25
EPS = 1e-6

LANES = 128
SUBLANES = 8
VMEM_LIMIT = 56 * 1024 * 1024

F32 = jnp.float32
BF16 = jnp.bfloat16
NEG_INF = float("-inf")


def _params(*sem):
    return pltpu.CompilerParams(dimension_semantics=sem, vmem_limit_bytes=VMEM_LIMIT)


def _silu(x):
    return x * jax.nn.sigmoid(x)


def _rope_lanes(v, tab):
    half = QK_ROPE // 2
    c = tab[:, 0:LANES]
    sa = tab[:, LANES:2 * LANES]
    sb = tab[:, 2 * LANES:3 * LANES]
    return v * c + pltpu.roll(v, LANES - half, 1) * sa + pltpu.roll(v, half, 1) * sb


def _proj_kernel(*refs, n_w, n_col, n_row, epilogue, cast_x):
    x_ref = refs[0]
    w_refs = refs[1:1 + n_w]
    col_refs = refs[1 + n_w:1 + n_w + n_col]
    row_refs = refs[1 + n_w + n_col:1 + n_w + n_col + n_row]
    o_ref = refs[1 + n_w + n_col + n_row]
    if cast_x:
        xb_ref = refs[-1]

        @pl.when(pl.program_id(1) == 0)
        def _():
            xb_ref[...] = x_ref[...].astype(BF16)

        xb = xb_ref[...]
    else:
        xb = x_ref[...]
    accs = [jnp.dot(xb, w[...], preferred_element_type=F32) for w in w_refs]
    out = epilogue(accs, [c[...] for c in col_refs], [r[...] for r in row_refs])
    o_ref[...] = out.astype(o_ref.dtype)


def _proj(x, ws, epilogue, out_dtype, *, tm, tn, out_tn=None, xcol=0, xk=None,
          col_aux=(), row_aux=(), name):
    m = x.shape[0]
    xk = x.shape[1] if xk is None else xk
    n = ws[0].shape[1]
    tm = min(tm, m)
    tn = min(tn, n)
    out_tn = tn if out_tn is None else out_tn
    nj = n // tn
    cast_x = x.dtype != BF16
    in_specs = [pl.BlockSpec((tm, xk), lambda i, j: (i, xcol))]
    in_specs += [pl.BlockSpec((xk, tn), lambda i, j: (0, j)) for _ in ws]
    in_specs += [pl.BlockSpec((1, tn), lambda i, j: (0, j)) for _ in col_aux]
    in_specs += [pl.BlockSpec((tm, r.shape[1]), lambda i, j: (i, 0)) for r in row_aux]
    kern = functools.partial(_proj_kernel, n_w=len(ws), n_col=len(col_aux), n_row=len(row_aux),
                             epilogue=epilogue, cast_x=cast_x)
    return pl.pallas_call(
        kern,
        out_shape=jax.ShapeDtypeStruct((m, nj * out_tn), out_dtype),
        grid=(m // tm, nj),
        in_specs=in_specs,
        out_specs=pl.BlockSpec((tm, out_tn), lambda i, j: (i, j)),
        scratch_shapes=[pltpu.VMEM((tm, xk), BF16)] if cast_x else [],
        compiler_params=_params("parallel", "arbitrary"),
        name=name,
    )(x, *ws, *col_aux, *row_aux)


def _ep_identity(accs, cols, rows):
    return accs[0]


def _ep_silu(accs, cols, rows):
    return _silu(accs[0])


def _ep_rmsnorm(accs, cols, rows):
    a = accs[0]
    return a * lax.rsqrt(jnp.mean(a * a, axis=-1, keepdims=True) + EPS) * cols[0]


def _ep_krope(accs, cols, rows):
    return _rope_lanes(accs[0], rows[0])


def _ep_q(accs, cols, rows):
    a = accs[0]
    scale = MLA_QK ** -0.5
    parts = []
    for h in range(a.shape[1] // QK_PAD):
        nope = a[:, h * QK_PAD:h * QK_PAD + QK_NOPE]
        rp = a[:, h * QK_PAD + QK_NOPE:(h + 1) * QK_PAD]
        parts += [nope * scale, _rope_lanes(rp, rows[0]) * scale]
    return jnp.concatenate(parts, axis=1)


def _ep_k(accs, cols, rows):
    a = accs[0]
    kr = rows[0].astype(F32)
    parts = []
    for h in range(a.shape[1] // QK_NOPE):
        parts += [a[:, h * QK_NOPE:(h + 1) * QK_NOPE], kr]
    return jnp.concatenate(parts, axis=1)


def _ep_outgate(accs, cols, rows):
    return jax.nn.sigmoid(accs[0]) * _silu(accs[1])


def _ep_bias(accs, cols, rows):
    return accs[0] + cols[0]


def _attn_kernel(q_ref, k_ref, v_ref, g_ref, o_ref, m_ref, l_ref, acc_ref, *, tq):
    i = pl.program_id(2)
    q = q_ref[0]
    m_ref[...] = jnp.full(m_ref.shape, NEG_INF, F32)
    l_ref[...] = jnp.zeros(l_ref.shape, F32)
    acc_ref[...] = jnp.zeros(acc_ref.shape, F32)

    def step(j, masked):
        start = pl.multiple_of(j * tq, tq)
        k = k_ref[0, pl.ds(start, tq), :]
        v = v_ref[0, pl.ds(start, tq), :]
        s = lax.dot_general(q, k, (((1,), (1,)), ((), ())), preferred_element_type=F32)
        if masked:
            r = lax.broadcasted_iota(jnp.int32, s.shape, 0) // CHUNK
            c = lax.broadcasted_iota(jnp.int32, s.shape, 1) // CHUNK
            s = jnp.where(c <= r, s, NEG_INF)
        m_prev = m_ref[...]
        m_new = jnp.maximum(m_prev, jnp.max(s, axis=-1, keepdims=True))
        alpha = jnp.exp(m_prev - m_new)
        p = jnp.exp(s - m_new)
        l_ref[...] = alpha * l_ref[...] + jnp.sum(p, axis=-1, keepdims=True)
        acc_ref[...] = alpha * acc_ref[...] + jnp.dot(p.astype(BF16), v, preferred_element_type=F32)
        m_ref[...] = m_new

    def body(j, carry):
        step(j, False)
        return carry

    lax.fori_loop(0, i, body, 0)
    step(i, True)
    o = acc_ref[...] / l_ref[...]
    o_ref[0] = (o * g_ref[0].astype(F32)).astype(o_ref.dtype)


def _attention(q, k, v, gz, *, tq):
    b, s, _ = q.shape
    tq = min(tq, s)
    return pl.pallas_call(
        functools.partial(_attn_kernel, tq=tq),
        out_shape=jax.ShapeDtypeStruct((b, s, MLA_WIDTH), BF16),
        grid=(b, MLA_HEADS, s // tq),
        in_specs=[
            pl.BlockSpec((1, tq, QK_PAD), lambda b_, h, i: (b_, i, h)),
            pl.BlockSpec((1, s, QK_PAD), lambda b_, h, i: (b_, 0, h)),
            pl.BlockSpec((1, s, V_HEAD), lambda b_, h, i: (b_, 0, h)),
            pl.BlockSpec((1, tq, V_HEAD), lambda b_, h, i: (b_, i, h)),
        ],
        out_specs=pl.BlockSpec((1, tq, V_HEAD), lambda b_, h, i: (b_, i, h)),
        scratch_shapes=[pltpu.VMEM((tq, 1), F32), pltpu.VMEM((tq, 1), F32),
                        pltpu.VMEM((tq, V_HEAD), F32)],
        compiler_params=_params("parallel", "parallel", "arbitrary"),
        name="mla_attention",
    )(q, k, v, gz)


def _out_ln_kernel(a_ref, x_ref, w_ref, lw_ref, lb_ref, o_ref):
    y = jnp.dot(a_ref[...], w_ref[...], preferred_element_type=F32)
    t = ALPHA * x_ref[...] + y
    mu = jnp.mean(t, axis=-1, keepdims=True)
    d = t - mu
    var = jnp.mean(d * d, axis=-1, keepdims=True)
    o_ref[...] = d * lax.rsqrt(var + EPS) * lw_ref[...] + lb_ref[...]


def _out_ln(a, x, w, lw, lb, *, tm):
    m = x.shape[0]
    tm = min(tm, m)
    return pl.pallas_call(
        _out_ln_kernel,
        out_shape=jax.ShapeDtypeStruct((m, D_MODEL), F32),
        grid=(m // tm,),
        in_specs=[
            pl.BlockSpec((tm, a.shape[1]), lambda i: (i, 0)),
            pl.BlockSpec((tm, D_MODEL), lambda i: (i, 0)),
            pl.BlockSpec(w.shape, lambda i: (0, 0)),
            pl.BlockSpec((1, D_MODEL), lambda i: (0, 0)),
            pl.BlockSpec((1, D_MODEL), lambda i: (0, 0)),
        ],
        out_specs=pl.BlockSpec((tm, D_MODEL), lambda i: (i, 0)),
        compiler_params=_params("parallel"),
        name="out_proj_layernorm",
    )(a, x, w, lw, lb)


def _log_sigmoid(x):
    return jnp.minimum(x, 0.0) - jnp.log1p(jnp.exp(-jnp.abs(x)))


def _cell_kernel(raw_ref, v_ref, g_ref, og_ref, cw_ref, cb_ref, hn_ref, o_ref,
                 xpad_ref, c_ref, n_ref, m_ref, *, cl):
    @pl.when(pl.program_id(1) == 0)
    def _():
        xpad_ref[0:SUBLANES, :] = jnp.zeros((SUBLANES, xpad_ref.shape[1]), F32)
        c_ref[...] = jnp.zeros(c_ref.shape, F32)
        n_ref[...] = jnp.zeros(n_ref.shape, F32)
        m_ref[...] = jnp.zeros(m_ref.shape, F32)

    xpad_ref[SUBLANES:SUBLANES + cl, :] = raw_ref[0]

    g = g_ref[0]
    rows = lax.broadcasted_iota(jnp.int32, (cl, cl), 0)
    cols = lax.broadcasted_iota(jnp.int32, (cl, cl), 1)
    tril = cols <= rows
    bcum = jnp.dot(tril.astype(F32), _log_sigmoid(g), preferred_element_type=F32,
                   precision=lax.Precision.HIGHEST)
    g_t = g.T
    bcum_t = bcum.T

    for h in range(ML_HEADS):
        def conv_silu(c0):
            acc = cb_ref[:, c0:c0 + ML_QK]
            for tap in range(CONV_W):
                off = SUBLANES - (CONV_W - 1) + tap
                acc = acc + xpad_ref[off:off + cl, c0:c0 + ML_QK] * cw_ref[tap:tap + 1, c0:c0 + ML_QK]
            return _silu(acc)

        q = conv_silu(h * ML_QK)
        k = conv_silu(ML_QK_WIDTH + h * ML_QK) * (ML_QK ** -0.5)
        v = v_ref[0, :, h * ML_V:(h + 1) * ML_V]
        qb = q.astype(BF16)

        li_row = g_t[h:h + 1, :]
        li_col = g[:, h:h + 1]
        b_row = bcum_t[ML_HEADS + h:ML_HEADS + h + 1, :]
        b_col = bcum[:, ML_HEADS + h:ML_HEADS + h + 1]
        m_st = m_ref[h][:, 0:1]
        n_st = n_ref[h]
        c_st = c_ref[h]

        dmat = jnp.where(tril, b_col - b_row + li_row, NEG_INF)
        m_inter = b_col + m_st
        m_t = jnp.maximum(m_inter, jnp.max(dmat, axis=-1, keepdims=True))
        w_inter = jnp.exp(m_inter - m_t)
        s_qk = lax.dot_general(qb, k.astype(BF16), (((1,), (1,)), ((), ())),
                               preferred_element_type=F32) * jnp.exp(dmat - m_t)
        num = (w_inter * jnp.dot(qb, c_st.astype(BF16), preferred_element_type=F32)
               + jnp.dot(s_qk.astype(BF16), v, preferred_element_type=F32))
        den = (w_inter * jnp.sum(q * n_st, axis=-1, keepdims=True)
               + jnp.sum(s_qk, axis=-1, keepdims=True))
        h_out = num / jnp.maximum(jnp.abs(den), jnp.exp(-m_t))

        b_last = b_col[cl - 1:cl, :]
        g_col = b_last - b_col + li_col
        m_new = jnp.maximum(b_last + m_st, jnp.max(g_col, axis=0, keepdims=True))
        decay = jnp.exp(b_last + m_st - m_new)
        kw = k * jnp.exp(g_col - m_new)
        c_ref[h] = decay * c_st + jnp.dot(kw.T.astype(BF16), v, preferred_element_type=F32)
        n_ref[h] = decay * n_st + jnp.sum(kw, axis=0, keepdims=True)
        m_ref[h] = jnp.broadcast_to(m_new, (1, LANES))

        hn = h_out * lax.rsqrt(jnp.mean(h_out * h_out, axis=-1, keepdims=True) + EPS)
        hn = hn * hn_ref[:, h * ML_V:(h + 1) * ML_V]
        gate = og_ref[0, :, h * ML_V:(h + 1) * ML_V].astype(F32)
        o_ref[0, :, h * ML_V:(h + 1) * ML_V] = (hn * gate).astype(o_ref.dtype)

    xpad_ref[0:SUBLANES, :] = xpad_ref[cl:cl + SUBLANES, :]


def _mlstm_cell(raw, v, gates, og, conv_w, conv_b, head_norm, *, cl):
    b, s, _ = raw.shape
    cl = min(cl, s)
    return pl.pallas_call(
        functools.partial(_cell_kernel, cl=cl),
        out_shape=jax.ShapeDtypeStruct((b, s, ML_WIDTH), BF16),
        grid=(b, s // cl),
        in_specs=[
            pl.BlockSpec((1, cl, 2 * ML_QK_WIDTH), lambda b_, c: (b_, c, 0)),
            pl.BlockSpec((1, cl, ML_WIDTH), lambda b_, c: (b_, c, 0)),
            pl.BlockSpec((1, cl, GATE_PAD), lambda b_, c: (b_, c, 0)),
            pl.BlockSpec((1, cl, ML_WIDTH), lambda b_, c: (b_, c, 0)),
            pl.BlockSpec((CONV_W, 2 * ML_QK_WIDTH), lambda b_, c: (0, 0)),
            pl.BlockSpec((1, 2 * ML_QK_WIDTH), lambda b_, c: (0, 0)),
            pl.BlockSpec((1, ML_WIDTH), lambda b_, c: (0, 0)),
        ],
        out_specs=pl.BlockSpec((1, cl, ML_WIDTH), lambda b_, c: (b_, c, 0)),
        scratch_shapes=[
            pltpu.VMEM((cl + SUBLANES, 2 * ML_QK_WIDTH), F32),
            pltpu.VMEM((ML_HEADS, ML_QK, ML_V), F32),
            pltpu.VMEM((ML_HEADS, 1, ML_QK), F32),
            pltpu.VMEM((ML_HEADS, 1, LANES), F32),
        ],
        compiler_params=_params("parallel", "arbitrary"),
        name="mlstm_cell",
    )(raw, v, gates, og, conv_w, conv_b, head_norm)


def _rope_table(positions):
    half = QK_ROPE // 2
    inv_freq = ROPE_THETA ** (-jnp.arange(0, QK_ROPE, 2, dtype=F32) / QK_ROPE)
    ang = positions.astype(F32)[..., None] * inv_freq
    cos, sin = jnp.cos(ang), jnp.sin(ang)
    z = jnp.zeros_like(cos)
    tab = jnp.concatenate([cos, cos, z, z, -sin, z, z, z, z, sin, z, z], axis=-1)
    return tab.reshape(-1, 12 * half)


def _mla_layer(x, rope_tab, bsz, seq, w_in, q_norm, w_qb, kv_norm, w_kvb, w_out, lw, lb):
    o_kv = Q_LORA + KV_LORA
    o_z = o_kv + QK_ROPE
    w_c = w_in[:, :o_kv].astype(BF16)
    w_kr = jnp.pad(w_in[:, o_kv:o_z], ((0, 0), (0, LANES - QK_ROPE))).astype(BF16)
    w_z = w_in[:, o_z:].astype(BF16)
    norms = jnp.concatenate([q_norm, kv_norm]).reshape(1, o_kv)
    w_q = jnp.pad(w_qb.reshape(Q_LORA, MLA_HEADS, MLA_QK),
                  ((0, 0), (0, 0), (0, QK_PAD - MLA_QK))).reshape(Q_LORA, MLA_HEADS * QK_PAD).astype(BF16)
    w_kv3 = w_kvb.reshape(KV_LORA, MLA_HEADS, QK_NOPE + V_HEAD)
    w_k = w_kv3[:, :, :QK_NOPE].reshape(KV_LORA, MLA_HEADS * QK_NOPE).astype(BF16)
    w_v = w_kv3[:, :, QK_NOPE:].reshape(KV_LORA, MLA_WIDTH).astype(BF16)

    c_n = _proj(x, [w_c], _ep_rmsnorm, BF16, tm=1024, tn=Q_LORA, col_aux=[norms], name="mla_latent")
    kr = _proj(x, [w_kr], _ep_krope, BF16, tm=1024, tn=LANES, row_aux=[rope_tab], name="mla_krope")
    gz = _proj(x, [w_z], _ep_silu, BF16, tm=1024, tn=512, name="mla_gate")
    q = _proj(c_n, [w_q], _ep_q, BF16, tm=1024, tn=1024, xcol=0, xk=Q_LORA, row_aux=[rope_tab],
              name="mla_q")
    k = _proj(c_n, [w_k], _ep_k, BF16, tm=1024, tn=1024, out_tn=2048, xcol=1, xk=KV_LORA,
              row_aux=[kr], name="mla_k")
    v = _proj(c_n, [w_v], _ep_identity, BF16, tm=1024, tn=1024, xcol=1, xk=KV_LORA, name="mla_v")

    shape3 = lambda t: t.reshape(bsz, seq, t.shape[-1])
    a = _attention(shape3(q), shape3(k), shape3(v), shape3(gz), tq=512)
    return _out_ln(a.reshape(bsz * seq, MLA_WIDTH), x, w_out.astype(BF16),
                   lw.reshape(1, -1), lb.reshape(1, -1), tm=512)


def _mlstm_layer(x, bsz, seq, w_in, conv_w, conv_b, gate_b, head_norm, w_out, lw, lb):
    o1 = 2 * ML_QK_WIDTH
    o2 = o1 + ML_WIDTH
    o3 = o2 + ML_WIDTH
    o4 = o3 + 2 * ML_HEADS
    w_qk = w_in[:, :o1].astype(BF16)
    w_v = w_in[:, o1:o2].astype(BF16)
    w_o = w_in[:, o2:o3].astype(BF16)
    w_g = jnp.pad(w_in[:, o3:o4], ((0, 0), (0, GATE_PAD - 2 * ML_HEADS))).astype(BF16)
    w_z = w_in[:, o4:].astype(BF16)
    gb = jnp.pad(gate_b, (0, GATE_PAD - 2 * ML_HEADS)).reshape(1, GATE_PAD)

    raw = _proj(x, [w_qk], _ep_identity, F32, tm=1024, tn=512, name="ml_qk")
    v = _proj(x, [w_v], _ep_identity, BF16, tm=1024, tn=512, name="ml_v")
    og = _proj(x, [w_o, w_z], _ep_outgate, BF16, tm=1024, tn=512, name="ml_outgate")
    gates = _proj(x, [w_g], _ep_bias, F32, tm=1024, tn=GATE_PAD, col_aux=[gb], name="ml_gates")

    shape3 = lambda t: t.reshape(bsz, seq, t.shape[-1])
    cell = _mlstm_cell(shape3(raw), shape3(v), shape3(gates), shape3(og), conv_w,
                       conv_b.reshape(1, -1), head_norm.reshape(1, -1), cl=256)
    return _out_ln(cell.reshape(bsz * seq, ML_WIDTH), x, w_out.astype(BF16),
                   lw.reshape(1, -1), lb.reshape(1, -1), tm=512)


def kernel(x, positions, mla_w_in, mla_q_norm, mla_w_qb, mla_kv_norm, mla_w_kvb, mla_w_out,
           ml_w_in, ml_conv_w, ml_conv_b, ml_gate_b, ml_head_norm, ml_w_out, ln_w, ln_b):
    bsz, seq, _ = x.shape
    rope_tab = _rope_table(positions)
    h = x.reshape(bsz * seq, D_MODEL)
    for layer in range(DEPTH):
        j = layer // N_MIXERS
        if layer % N_MIXERS == 0:
            h = _mla_layer(h, rope_tab, bsz, seq, mla_w_in[j], mla_q_norm[j], mla_w_qb[j],
                           mla_kv_norm[j], mla_w_kvb[j], mla_w_out[j], ln_w[layer], ln_b[layer])
        else:
            h = _mlstm_layer(h, bsz, seq, ml_w_in[j], ml_conv_w[j], ml_conv_b[j], ml_gate_b[j],
                             ml_head_norm[j], ml_w_out[j], ln_w[layer], ln_b[layer])
    return h.reshape(bsz, seq, D_MODEL)
```

```python
import functools

import jax
import jax.numpy as jnp
from jax import lax
from jax.experimental import pallas as pl
from jax.experimental.pallas import tpu as pltpu

D_MODEL = 2048
DEPTH = 4
CHUNK = 64
N_MIXERS = 2

MLA_HEADS = 16
QK_NOPE = 128
QK_ROPE = 64
V_HEAD = 128
Q_LORA = 512
KV_LORA = 512
ROPE_THETA = 10000.0
MLA_QK = QK_NOPE + QK_ROPE
MLA_WIDTH = MLA_HEADS * V_HEAD
QK_PAD = 256

ML_HEADS = 8
ML_QK = 128
ML_V = 256
ML_QK_WIDTH = ML_HEADS * ML_QK
ML_WIDTH = ML_HEADS * ML_V
CONV_W = 4
GATE_PAD = 128

ALPHA = (2 * DEPTH) ** 0.25
EPS = 1e-6

LANES = 128
SUBLANES = 8
VMEM_LIMIT = 56 * 1024 * 1024

F32 = jnp.float32
BF16 = jnp.bfloat16
NEG_INF = float("-inf")
LOG2E = 1.4426950408889634

ATTN_TQ = 1024


def _params(*sem):
    return pltpu.CompilerParams(dimension_semantics=sem, vmem_limit_bytes=VMEM_LIMIT)


def _silu(x):
    return x * jax.nn.sigmoid(x)


def _rope_lanes(v, tab):
    half = QK_ROPE // 2
    c = tab[:, 0:LANES]
    sa = tab[:, LANES:2 * LANES]
    sb = tab[:, 2 * LANES:3 * LANES]
    return v * c + pltpu.roll(v, LANES - half, 1) * sa + pltpu.roll(v, half, 1) * sb


def _proj_kernel(*refs, n_w, n_col, n_row, epilogue, cast_x):
    x_ref = refs[0]
    w_refs = refs[1:1 + n_w]
    col_refs = refs[1 + n_w:1 + n_w + n_col]
    row_refs = refs[1 + n_w + n_col:1 + n_w + n_col + n_row]
    o_ref = refs[1 + n_w + n_col + n_row]
    if cast_x:
        xb_ref = refs[-1]

        @pl.when(pl.program_id(1) == 0)
        def _():
            xb_ref[...] = x_ref[...].astype(BF16)

        xb = xb_ref[...]
    else:
        xb = x_ref[...]
    accs = [jnp.dot(xb, w[...], preferred_element_type=F32) for w in w_refs]
    out = epilogue(accs, [c[...] for c in col_refs], [r[...] for r in row_refs])
    o_ref[...] = out.astype(o_ref.dtype)


def _proj(x, ws, epilogue, out_dtype, *, tm, tn, out_tn=None, xcol=0, xk=None,
          col_aux=(), row_aux=(), name):
    m = x.shape[0]
    xk = x.shape[1] if xk is None else xk
    n = ws[0].shape[1]
    tm = min(tm, m)
    tn = min(tn, n)
    out_tn = tn if out_tn is None else out_tn
    nj = n // tn
    cast_x = x.dtype != BF16
    in_specs = [pl.BlockSpec((tm, xk), lambda i, j: (i, xcol))]
    in_specs += [pl.BlockSpec((xk, tn), lambda i, j: (0, j)) for _ in ws]
    in_specs += [pl.BlockSpec((1, tn), lambda i, j: (0, j)) for _ in col_aux]
    in_specs += [pl.BlockSpec((tm, r.shape[1]), lambda i, j: (i, 0)) for r in row_aux]
    kern = functools.partial(_proj_kernel, n_w=len(ws), n_col=len(col_aux), n_row=len(row_aux),
                             epilogue=epilogue, cast_x=cast_x)
    return pl.pallas_call(
        kern,
        out_shape=jax.ShapeDtypeStruct((m, nj * out_tn), out_dtype),
        grid=(m // tm, nj),
        in_specs=in_specs,
        out_specs=pl.BlockSpec((tm, out_tn), lambda i, j: (i, j)),
        scratch_shapes=[pltpu.VMEM((tm, xk), BF16)] if cast_x else [],
        compiler_params=_params("parallel", "arbitrary"),
        name=name,
    )(x, *ws, *col_aux, *row_aux)


def _ep_identity(accs, cols, rows):
    return accs[0]


def _ep_silu(accs, cols, rows):
    return _silu(accs[0])


def _ep_rmsnorm(accs, cols, rows):
    a = accs[0]
    return a * lax.rsqrt(jnp.mean(a * a, axis=-1, keepdims=True) + EPS) * cols[0]


def _ep_krope(accs, cols, rows):
    return _rope_lanes(accs[0], rows[0])


def _ep_q(accs, cols, rows):
    a = accs[0]
    scale = MLA_QK ** -0.5 * LOG2E
    parts = []
    for h in range(a.shape[1] // QK_PAD):
        nope = a[:, h * QK_PAD:h * QK_PAD + QK_NOPE]
        rp = a[:, h * QK_PAD + QK_NOPE:(h + 1) * QK_PAD]
        parts += [nope * scale, _rope_lanes(rp, rows[0]) * scale]
    return jnp.concatenate(parts, axis=1)


def _ep_k(accs, cols, rows):
    a = accs[0]
    kr = rows[0].astype(F32)
    parts = []
    for h in range(a.shape[1] // QK_NOPE):
        parts += [a[:, h * QK_NOPE:(h + 1) * QK_NOPE], kr]
    return jnp.concatenate(parts, axis=1)


def _ep_outgate(accs, cols, rows):
    return jax.nn.sigmoid(accs[0]) * _silu(accs[1])


def _ep_bias(accs, cols, rows):
    return accs[0] + cols[0]


def _attn_kernel(q_ref, k_ref, vt_ref, g_ref, o_ref, s0_ref, s1_ref, m_ref, l_ref, acc_ref, *, tq, tk):
    i = pl.program_id(2)
    q = q_ref[0]
    m_ref[...] = jnp.full(m_ref.shape, NEG_INF, F32)
    l_ref[...] = jnp.zeros(l_ref.shape, F32)
    acc_ref[...] = jnp.zeros(acc_ref.shape, F32)

    def scores(j, s_ref, c0=0):
        start = pl.multiple_of(j * tk, tk)
        s_ref[:, c0:] = lax.dot_general(k_ref[0, pl.ds(start, tk), :], q[c0:], (((1,), (1,)), ((), ())),
                                        preferred_element_type=F32)

    def update(j, s_ref, masked, c0=0):
        s = s_ref[:, c0:]
        if masked:
            r = lax.broadcasted_iota(jnp.int32, s.shape, 0) // CHUNK
            c = lax.broadcasted_iota(jnp.int32, s.shape, 1) // CHUNK
            s = jnp.where(r <= c, s, NEG_INF)
        m_prev = m_ref[:, c0:]
        m_new = jnp.maximum(m_prev, jnp.max(s, axis=0, keepdims=True))
        alpha = jnp.exp2(m_prev - m_new)
        p = jnp.exp2(s - m_new)
        l_ref[:, c0:] = alpha * l_ref[:, c0:] + jnp.sum(p, axis=0, keepdims=True)
        acc_ref[:, c0:] = alpha * acc_ref[:, c0:] + jnp.dot(vt_ref[0, j], p.astype(BF16),
                                                            preferred_element_type=F32)
        m_ref[:, c0:] = m_new

    scores(0, s0_ref)

    def body(jj, carry):
        scores(2 * jj + 1, s1_ref)
        update(2 * jj, s0_ref, False)
        scores(2 * jj + 2, s0_ref)
        update(2 * jj + 1, s1_ref, False)
        return carry

    lax.fori_loop(0, i, body, 0)
    scores(2 * i + 1, s1_ref, tk)
    update(2 * i, s0_ref, True)
    update(2 * i + 1, s1_ref, True, tk)
    o = (acc_ref[...] / l_ref[...]).T
    o_ref[0] = (o * g_ref[0].astype(F32)).astype(o_ref.dtype)


def _attention(q, k, vt, gz, *, tq):
    b, s, _ = q.shape
    tk = vt.shape[-1]
    assert tq == 2 * tk and s % tq == 0
    return pl.pallas_call(
        functools.partial(_attn_kernel, tq=tq, tk=tk),
        out_shape=jax.ShapeDtypeStruct((b, s, MLA_WIDTH), BF16),
        grid=(b, MLA_HEADS, s // tq),
        in_specs=[
            pl.BlockSpec((1, tq, QK_PAD), lambda b_, h, i: (b_, i, h)),
            pl.BlockSpec((1, s, QK_PAD), lambda b_, h, i: (b_, 0, h)),
            pl.BlockSpec((1, s // tk, V_HEAD, tk), lambda b_, h, i: (b_, 0, h, 0)),
            pl.BlockSpec((1, tq, V_HEAD), lambda b_, h, i: (b_, i, h)),
        ],
        out_specs=pl.BlockSpec((1, tq, V_HEAD), lambda b_, h, i: (b_, i, h)),
        scratch_shapes=[pltpu.VMEM((tk, tq), F32), pltpu.VMEM((tk, tq), F32),
                        pltpu.VMEM((1, tq), F32), pltpu.VMEM((1, tq), F32),
                        pltpu.VMEM((V_HEAD, tq), F32)],
        compiler_params=_params("parallel", "parallel", "arbitrary"),
        name="mla_attention",
    )(q, k, vt, gz)


def _vt_kernel(w_ref, c_ref, o_ref):
    o_ref[0, 0] = lax.dot_general(w_ref[...], c_ref[0], (((1,), (1,)), ((), ())),
                                  preferred_element_type=F32).astype(o_ref.dtype)


def _values_transposed(c_n, w_vt, *, tk):
    b, s, _ = c_n.shape
    return pl.pallas_call(
        _vt_kernel,
        out_shape=jax.ShapeDtypeStruct((b, s // tk, MLA_WIDTH, tk), BF16),
        grid=(b, s // tk),
        in_specs=[
            pl.BlockSpec((MLA_WIDTH, KV_LORA), lambda b_, j: (0, 0)),
            pl.BlockSpec((1, tk, KV_LORA), lambda b_, j: (b_, j, 1)),
        ],
        out_specs=pl.BlockSpec((1, 1, MLA_WIDTH, tk), lambda b_, j: (b_, j, 0, 0)),
        compiler_params=_params("parallel", "parallel"),
        name="mla_vt",
    )(w_vt, c_n)


def _out_ln_kernel(a_ref, x_ref, w_ref, lw_ref, lb_ref, o_ref):
    y = jnp.dot(a_ref[...], w_ref[...], preferred_element_type=F32)
    t = ALPHA * x_ref[...] + y
    mu = jnp.mean(t, axis=-1, keepdims=True)
    d = t - mu
    var = jnp.mean(d * d, axis=-1, keepdims=True)
    o_ref[...] = d * lax.rsqrt(var + EPS) * lw_ref[...] + lb_ref[...]


def _out_ln(a, x, w, lw, lb, *, tm):
    m = x.shape[0]
    tm = min(tm, m)
    return pl.pallas_call(
        _out_ln_kernel,
        out_shape=jax.ShapeDtypeStruct((m, D_MODEL), F32),
        grid=(m // tm,),
        in_specs=[
            pl.BlockSpec((tm, a.shape[1]), lambda i: (i, 0)),
            pl.BlockSpec((tm, D_MODEL), lambda i: (i, 0)),
            pl.BlockSpec(w.shape, lambda i: (0, 0)),
            pl.BlockSpec((1, D_MODEL), lambda i: (0, 0)),
            pl.BlockSpec((1, D_MODEL), lambda i: (0, 0)),
        ],
        out_specs=pl.BlockSpec((tm, D_MODEL), lambda i: (i, 0)),
        compiler_params=_params("parallel"),
        name="out_proj_layernorm",
    )(a, x, w, lw, lb)


def _log_sigmoid(x):
    return jnp.minimum(x, 0.0) - jnp.log1p(jnp.exp(-jnp.abs(x)))


def _cell_kernel(raw_ref, v_ref, g_ref, og_ref, cw_ref, cb_ref, hn_ref, o_ref,
                 xpad_ref, c_ref, n_ref, m_ref, *, cl):
    @pl.when(pl.program_id(1) == 0)
    def _():
        xpad_ref[0:SUBLANES, :] = jnp.zeros((SUBLANES, xpad_ref.shape[1]), F32)
        c_ref[...] = jnp.zeros(c_ref.shape, F32)
        n_ref[...] = jnp.zeros(n_ref.shape, F32)
        m_ref[...] = jnp.zeros(m_ref.shape, F32)

    xpad_ref[SUBLANES:SUBLANES + cl, :] = raw_ref[0]

    g = g_ref[0]
    rows = lax.broadcasted_iota(jnp.int32, (cl, cl), 0)
    cols = lax.broadcasted_iota(jnp.int32, (cl, cl), 1)
    tril = cols <= rows
    bcum = jnp.dot(tril.astype(F32), _log_sigmoid(g), preferred_element_type=F32,
                   precision=lax.Precision.HIGHEST)
    g_t = g.T
    bcum_t = bcum.T

    for h in range(ML_HEADS):
        def conv_silu(c0):
            acc = cb_ref[:, c0:c0 + ML_QK]
            for tap in range(CONV_W):
                off = SUBLANES - (CONV_W - 1) + tap
                acc = acc + xpad_ref[off:off + cl, c0:c0 + ML_QK] * cw_ref[tap:tap + 1, c0:c0 + ML_QK]
            return _silu(acc)

        q = conv_silu(h * ML_QK)
        k = conv_silu(ML_QK_WIDTH + h * ML_QK) * (ML_QK ** -0.5)
        v = v_ref[0, :, h * ML_V:(h + 1) * ML_V]
        qb = q.astype(BF16)

        li_row = g_t[h:h + 1, :]
        li_col = g[:, h:h + 1]
        b_row = bcum_t[ML_HEADS + h:ML_HEADS + h + 1, :]
        b_col = bcum[:, ML_HEADS + h:ML_HEADS + h + 1]
        m_st = m_ref[h][:, 0:1]
        n_st = n_ref[h]
        c_st = c_ref[h]

        dmat = jnp.where(tril, b_col - b_row + li_row, NEG_INF)
        m_inter = b_col + m_st
        m_t = jnp.maximum(m_inter, jnp.max(dmat, axis=-1, keepdims=True))
        w_inter = jnp.exp(m_inter - m_t)
        s_qk = lax.dot_general(qb, k.astype(BF16), (((1,), (1,)), ((), ())),
                               preferred_element_type=F32) * jnp.exp(dmat - m_t)
        num = (w_inter * jnp.dot(qb, c_st.astype(BF16), preferred_element_type=F32)
               + jnp.dot(s_qk.astype(BF16), v, preferred_element_type=F32))
        den = (w_inter * jnp.sum(q * n_st, axis=-1, keepdims=True)
               + jnp.sum(s_qk, axis=-1, keepdims=True))
        h_out = num / jnp.maximum(jnp.abs(den), jnp.exp(-m_t))

        b_last = b_col[cl - 1:cl, :]
        g_col = b_last - b_col + li_col
        m_new = jnp.maximum(b_last + m_st, jnp.max(g_col, axis=0, keepdims=True))
        decay = jnp.exp(b_last + m_st - m_new)
        kw = k * jnp.exp(g_col - m_new)
        c_ref[h] = decay * c_st + jnp.dot(kw.T.astype(BF16), v, preferred_element_type=F32)
        n_ref[h] = decay * n_st + jnp.sum(kw, axis=0, keepdims=True)
        m_ref[h] = jnp.broadcast_to(m_new, (1, LANES))

        hn = h_out * lax.rsqrt(jnp.mean(h_out * h_out, axis=-1, keepdims=True) + EPS)
        hn = hn * hn_ref[:, h * ML_V:(h + 1) * ML_V]
        gate = og_ref[0, :, h * ML_V:(h + 1) * ML_V].astype(F32)
        o_ref[0, :, h * ML_V:(h + 1) * ML_V] = (hn * gate).astype(o_ref.dtype)

    xpad_ref[0:SUBLANES, :] = xpad_ref[cl:cl + SUBLANES, :]


def _mlstm_cell(raw, v, gates, og, conv_w, conv_b, head_norm, *, cl):
    b, s, _ = raw.shape
    cl = min(cl, s)
    return pl.pallas_call(
        functools.partial(_cell_kernel, cl=cl),
        out_shape=jax.ShapeDtypeStruct((b, s, ML_WIDTH), BF16),
        grid=(b, s // cl),
        in_specs=[
            pl.BlockSpec((1, cl, 2 * ML_QK_WIDTH), lambda b_, c: (b_, c, 0)),
            pl.BlockSpec((1, cl, ML_WIDTH), lambda b_, c: (b_, c, 0)),
            pl.BlockSpec((1, cl, GATE_PAD), lambda b_, c: (b_, c, 0)),
            pl.BlockSpec((1, cl, ML_WIDTH), lambda b_, c: (b_, c, 0)),
            pl.BlockSpec((CONV_W, 2 * ML_QK_WIDTH), lambda b_, c: (0, 0)),
            pl.BlockSpec((1, 2 * ML_QK_WIDTH), lambda b_, c: (0, 0)),
            pl.BlockSpec((1, ML_WIDTH), lambda b_, c: (0, 0)),
        ],
        out_specs=pl.BlockSpec((1, cl, ML_WIDTH), lambda b_, c: (b_, c, 0)),
        scratch_shapes=[
            pltpu.VMEM((cl + SUBLANES, 2 * ML_QK_WIDTH), F32),
            pltpu.VMEM((ML_HEADS, ML_QK, ML_V), F32),
            pltpu.VMEM((ML_HEADS, 1, ML_QK), F32),
            pltpu.VMEM((ML_HEADS, 1, LANES), F32),
        ],
        compiler_params=_params("parallel", "arbitrary"),
        name="mlstm_cell",
    )(raw, v, gates, og, conv_w, conv_b, head_norm)


def _rope_table(positions):
    half = QK_ROPE // 2
    inv_freq = ROPE_THETA ** (-jnp.arange(0, QK_ROPE, 2, dtype=F32) / QK_ROPE)
    ang = positions.astype(F32)[..., None] * inv_freq
    cos, sin = jnp.cos(ang), jnp.sin(ang)
    z = jnp.zeros_like(cos)
    tab = jnp.concatenate([cos, cos, z, z, -sin, z, z, z, z, sin, z, z], axis=-1)
    return tab.reshape(-1, 12 * half)


def _mla_layer(x, rope_tab, bsz, seq, w_in, q_norm, w_qb, kv_norm, w_kvb, w_out, lw, lb):
    o_kv = Q_LORA + KV_LORA
    o_z = o_kv + QK_ROPE
    w_c = w_in[:, :o_kv].astype(BF16)
    w_kr = jnp.pad(w_in[:, o_kv:o_z], ((0, 0), (0, LANES - QK_ROPE))).astype(BF16)
    w_z = w_in[:, o_z:].astype(BF16)
    norms = jnp.concatenate([q_norm, kv_norm]).reshape(1, o_kv)
    w_q = jnp.pad(w_qb.reshape(Q_LORA, MLA_HEADS, MLA_QK),
                  ((0, 0), (0, 0), (0, QK_PAD - MLA_QK))).reshape(Q_LORA, MLA_HEADS * QK_PAD).astype(BF16)
    w_kv3 = w_kvb.reshape(KV_LORA, MLA_HEADS, QK_NOPE + V_HEAD)
    w_k = w_kv3[:, :, :QK_NOPE].reshape(KV_LORA, MLA_HEADS * QK_NOPE).astype(BF16)
    w_vt = w_kv3[:, :, QK_NOPE:].reshape(KV_LORA, MLA_WIDTH).T.astype(BF16)
    tq = min(ATTN_TQ, seq)

    c_n = _proj(x, [w_c], _ep_rmsnorm, BF16, tm=1024, tn=Q_LORA, col_aux=[norms], name="mla_latent")
    kr = _proj(x, [w_kr], _ep_krope, BF16, tm=1024, tn=LANES, row_aux=[rope_tab], name="mla_krope")
    gz = _proj(x, [w_z], _ep_silu, BF16, tm=1024, tn=1024, name="mla_gate")
    q = _proj(c_n, [w_q], _ep_q, BF16, tm=1024, tn=1024, xcol=0, xk=Q_LORA, row_aux=[rope_tab],
              name="mla_q")
    k = _proj(c_n, [w_k], _ep_k, BF16, tm=1024, tn=1024, out_tn=2048, xcol=1, xk=KV_LORA,
              row_aux=[kr], name="mla_k")

    shape3 = lambda t: t.reshape(bsz, seq, t.shape[-1])
    vt = _values_transposed(shape3(c_n), w_vt, tk=tq // 2)
    a = _attention(shape3(q), shape3(k), vt, shape3(gz), tq=tq)
    return _out_ln(a.reshape(bsz * seq, MLA_WIDTH), x, w_out.astype(BF16),
                   lw.reshape(1, -1), lb.reshape(1, -1), tm=512)


def _mlstm_layer(x, bsz, seq, w_in, conv_w, conv_b, gate_b, head_norm, w_out, lw, lb):
    o1 = 2 * ML_QK_WIDTH
    o2 = o1 + ML_WIDTH
    o3 = o2 + ML_WIDTH
    o4 = o3 + 2 * ML_HEADS
    w_qk = w_in[:, :o1].astype(BF16)
    w_v = w_in[:, o1:o2].astype(BF16)
    w_o = w_in[:, o2:o3].astype(BF16)
    w_g = jnp.pad(w_in[:, o3:o4], ((0, 0), (0, GATE_PAD - 2 * ML_HEADS))).astype(BF16)
    w_z = w_in[:, o4:].astype(BF16)
    gb = jnp.pad(gate_b, (0, GATE_PAD - 2 * ML_HEADS)).reshape(1, GATE_PAD)

    raw = _proj(x, [w_qk], _ep_identity, F32, tm=1024, tn=1024, name="ml_qk")
    v = _proj(x, [w_v], _ep_identity, BF16, tm=1024, tn=1024, name="ml_v")
    og = _proj(x, [w_o, w_z], _ep_outgate, BF16, tm=1024, tn=1024, name="ml_outgate")
    gates = _proj(x, [w_g], _ep_bias, F32, tm=1024, tn=GATE_PAD, col_aux=[gb], name="ml_gates")

    shape3 = lambda t: t.reshape(bsz, seq, t.shape[-1])
    cell = _mlstm_cell(shape3(raw), shape3(v), shape3(gates), shape3(og), conv_w,
                       conv_b.reshape(1, -1), head_norm.reshape(1, -1), cl=256)
    return _out_ln(cell.reshape(bsz * seq, ML_WIDTH), x, w_out.astype(BF16),
                   lw.reshape(1, -1), lb.reshape(1, -1), tm=512)


def kernel(x, positions, mla_w_in, mla_q_norm, mla_w_qb, mla_kv_norm, mla_w_kvb, mla_w_out,
           ml_w_in, ml_conv_w, ml_conv_b, ml_gate_b, ml_head_norm, ml_w_out, ln_w, ln_b):
    bsz, seq, _ = x.shape
    rope_tab = _rope_table(positions)
    h = x.reshape(bsz * seq, D_MODEL)
    for layer in range(DEPTH):
        j = layer // N_MIXERS
        if layer % N_MIXERS == 0:
            h = _mla_layer(h, rope_tab, bsz, seq, mla_w_in[j], mla_q_norm[j], mla_w_qb[j],
                           mla_kv_norm[j], mla_w_kvb[j], mla_w_out[j], ln_w[layer], ln_b[layer])
        else:
            h = _mlstm_layer(h, bsz, seq, ml_w_in[j], ml_conv_w[j], ml_conv_b[j], ml_gate_b[j],
                             ml_head_norm[j], ml_w_out[j], ln_w[layer], ln_b[layer])
    return h.reshape(bsz, seq, D_MODEL)
```

```python
import functools

import jax
import jax.numpy as jnp
from jax import lax
from jax.experimental import pallas as pl
from jax.experimental.pallas import tpu as pltpu

D_MODEL = 2048
DEPTH = 4
CHUNK = 64
N_MIXERS = 2

MLA_HEADS = 16
QK_NOPE = 128
QK_ROPE = 64
V_HEAD = 128
Q_LORA = 512
KV_LORA = 512
ROPE_THETA = 10000.0
MLA_QK = QK_NOPE + QK_ROPE
MLA_WIDTH = MLA_HEADS * V_HEAD
QK_PAD = 256
ML_HEADS = 8
ML_QK = 128
ML_V = 256
ML_QK_WIDTH = ML_HEADS * ML_QK
ML_WIDTH = ML_HEADS * ML_V
CONV_W = 4
GATE_PAD = 128

ALPHA = (2 * DEPTH) ** 0.25
EPS = 1e-6

LANES = 128
SUBLANES = 8
VMEM_LIMIT = 56 * 1024 * 1024

F32 = jnp.float32
BF16 = jnp.bfloat16
NEG_INF = float("-inf")
LOG2E = 1.4426950408889634

ATTN_TQ = 2048
ATTN_TK = 512
ATTN_STRIP = 256


def _params(*sem):
    return pltpu.CompilerParams(dimension_semantics=sem, vmem_limit_bytes=VMEM_LIMIT)


def _wide_tiles(x):
    return (2048, 2048) if x.dtype == BF16 else (1024, 1024)


def _silu(x):
    return x * jax.nn.sigmoid(x)


def _rope_lanes(v, tab):
    half = QK_ROPE // 2
    c = tab[:, 0:LANES]
    sa = tab[:, LANES:2 * LANES]
    sb = tab[:, 2 * LANES:3 * LANES]
    return v * c + pltpu.roll(v, LANES - half, 1) * sa + pltpu.roll(v, half, 1) * sb


def _proj_kernel(*refs, n_w, n_col, n_row, epilogue, cast_x, side_epilogue):
    x_ref = refs[0]
    w_refs = refs[1:1 + n_w]
    col_refs = refs[1 + n_w:1 + n_w + n_col]
    row_refs = refs[1 + n_w + n_col:1 + n_w + n_col + n_row]
    pos = 1 + n_w + n_col + n_row
    if side_epilogue is not None:
        sw_ref, sc_ref = refs[pos:pos + 2]
        pos += 2
    o_ref = refs[pos]
    first = pl.program_id(1) == 0
    if cast_x:
        xb_ref = refs[-1]

        @pl.when(first)
        def _():
            xb_ref[...] = x_ref[...].astype(BF16)

        xb = xb_ref[...]
    else:
        xb = x_ref[...]
    rows = [r[...] for r in row_refs]
    accs = [jnp.dot(xb, w[...], preferred_element_type=F32) for w in w_refs]
    o_ref[...] = epilogue(accs, [c[...] for c in col_refs], rows).astype(o_ref.dtype)
    if side_epilogue is not None:
        so_ref = refs[pos + 1]

        @pl.when(first)
        def _():
            acc = jnp.dot(xb, sw_ref[...], preferred_element_type=F32)
            so_ref[...] = side_epilogue([acc], [sc_ref[...]], rows).astype(so_ref.dtype)


def _proj(x, ws, epilogue, out_dtype, *, tm, tn, out_tn=None, xcol=0, xk=None,
          col_aux=(), row_aux=(), side=None, name):
    m = x.shape[0]
    xk = x.shape[1] if xk is None else xk
    n = ws[0].shape[1]
    tm = min(tm, m)
    tn = min(tn, n)
    out_tn = tn if out_tn is None else out_tn
    nj = n // tn
    cast_x = x.dtype != BF16
    in_specs = [pl.BlockSpec((tm, xk), lambda i, j: (i, xcol))]
    in_specs += [pl.BlockSpec((xk, tn), lambda i, j: (0, j)) for _ in ws]
    in_specs += [pl.BlockSpec((1, tn), lambda i, j: (0, j)) for _ in col_aux]
    in_specs += [pl.BlockSpec((tm, r.shape[1]), lambda i, j: (i, 0)) for r in row_aux]
    out_shape = jax.ShapeDtypeStruct((m, nj * out_tn), out_dtype)
    out_specs = pl.BlockSpec((tm, out_tn), lambda i, j: (i, j))
    side_args = ()
    if side is not None:
        w_side, col_side, side_epilogue, side_dtype = side
        ns = w_side.shape[1]
        in_specs += [pl.BlockSpec((xk, ns), lambda i, j: (0, 0)), pl.BlockSpec((1, ns), lambda i, j: (0, 0))]
        out_shape = (out_shape, jax.ShapeDtypeStruct((m, ns), side_dtype))
        out_specs = (out_specs, pl.BlockSpec((tm, ns), lambda i, j: (i, 0)))
        side_args = (w_side, col_side)
    kern = functools.partial(_proj_kernel, n_w=len(ws), n_col=len(col_aux), n_row=len(row_aux),
                             epilogue=epilogue, cast_x=cast_x,
                             side_epilogue=None if side is None else side[2])
    return pl.pallas_call(
        kern,
        out_shape=out_shape,
        grid=(m // tm, nj),
        in_specs=in_specs,
        out_specs=out_specs,
        scratch_shapes=[pltpu.VMEM((tm, xk), BF16)] if cast_x else [],
        compiler_params=_params("parallel", "arbitrary"),
        name=name,
    )(x, *ws, *col_aux, *row_aux, *side_args)


def _ep_identity(accs, cols, rows):
    return accs[0]


def _ep_silu(accs, cols, rows):
    return _silu(accs[0])


def _ep_rmsnorm(accs, cols, rows):
    a = accs[0]
    return a * lax.rsqrt(jnp.mean(a * a, axis=-1, keepdims=True) + EPS) * cols[0]


def _ep_krope(accs, cols, rows):
    return _rope_lanes(accs[0], rows[0])


def _ep_q(accs, cols, rows):
    a = accs[0]
    scale = MLA_QK ** -0.5 * LOG2E
    parts = []
    for h in range(a.shape[1] // QK_PAD):
        nope = a[:, h * QK_PAD:h * QK_PAD + QK_NOPE]
        rp = a[:, h * QK_PAD + QK_NOPE:(h + 1) * QK_PAD]
        parts += [nope * scale, _rope_lanes(rp, rows[0]) * scale]
    return jnp.concatenate(parts, axis=1)


def _ep_k(accs, cols, rows):
    a = accs[0]
    kr = rows[0].astype(F32)
    parts = []
    for h in range(a.shape[1] // QK_NOPE):
        parts += [a[:, h * QK_NOPE:(h + 1) * QK_NOPE], kr]
    return jnp.concatenate(parts, axis=1)


def _ep_outgate(accs, cols, rows):
    return jax.nn.sigmoid(accs[0]) * _silu(accs[1])


def _ep_bias(accs, cols, rows):
    return accs[0] + cols[0]


def _attn_kernel(q_ref, k_ref, vt_ref, g_ref, o_ref, s0_ref, s1_ref, m_ref, l_ref, acc_ref, *, tq, tk):
    i = pl.program_id(2)
    m_ref[...] = jnp.full(m_ref.shape, NEG_INF, F32)
    l_ref[...] = jnp.zeros(l_ref.shape, F32)
    acc_ref[...] = jnp.zeros(acc_ref.shape, F32)

    def scores(j, s_ref, c0=0):
        start = pl.multiple_of(j * tk, tk)
        kb = k_ref[0, pl.ds(start, tk), :]
        for c in range(c0, tq, ATTN_STRIP):
            s_ref[:, c:c + ATTN_STRIP] = lax.dot_general(
                kb, q_ref[0, c:c + ATTN_STRIP, :], (((1,), (1,)), ((), ())),
                preferred_element_type=F32)

    def update(j, s_ref, masked, c0=0):
        vt = vt_ref[0, j]
        for c in range(c0, tq, ATTN_STRIP):
            cs = slice(c, c + ATTN_STRIP)
            s = s_ref[:, cs]
            if masked and c < c0 + tk:
                r = lax.broadcasted_iota(jnp.int32, s.shape, 0) // CHUNK
                cc = (lax.broadcasted_iota(jnp.int32, s.shape, 1) + (c - c0)) // CHUNK
                s = jnp.where(r <= cc, s, NEG_INF)
            m_prev = m_ref[:, cs]
            m_new = jnp.maximum(m_prev, jnp.max(s, axis=0, keepdims=True))
            alpha = jnp.exp2(m_prev - m_new)
            p = jnp.exp2(s - m_new)
            l_ref[:, cs] = alpha * l_ref[:, cs] + jnp.sum(p, axis=0, keepdims=True)
            acc_ref[:, cs] = alpha * acc_ref[:, cs] + jnp.dot(vt, p.astype(BF16),
                                                              preferred_element_type=F32)
            m_ref[:, cs] = m_new

    nsub = tq // tk
    bufs = (s0_ref, s1_ref)
    scores(0, s0_ref)

    def body(jj, carry):
        for d in range(nsub):
            scores(nsub * jj + d + 1, bufs[(d + 1) % 2])
            update(nsub * jj + d, bufs[d % 2], False)
        return carry

    lax.fori_loop(0, i, body, 0)
    for d in range(nsub):
        if d + 1 < nsub:
            scores(nsub * i + d + 1, bufs[(d + 1) % 2], (d + 1) * tk)
        update(nsub * i + d, bufs[d % 2], True, d * tk)
    o = (acc_ref[...] / l_ref[...]).T
    o_ref[0] = (o * g_ref[0].astype(F32)).astype(o_ref.dtype)


def _attention(q, k, vt, gz, *, tq):
    b, s, _ = q.shape
    tk = vt.shape[-1]
    assert tq % (2 * tk) == 0 and s % tq == 0
    return pl.pallas_call(
        functools.partial(_attn_kernel, tq=tq, tk=tk),
        out_shape=jax.ShapeDtypeStruct((b, s, MLA_WIDTH), BF16),
        grid=(b, MLA_HEADS, s // tq),
        in_specs=[
            pl.BlockSpec((1, tq, QK_PAD), lambda b_, h, i: (b_, i, h)),
            pl.BlockSpec((1, s, QK_PAD), lambda b_, h, i: (b_, 0, h)),
            pl.BlockSpec((1, s // tk, V_HEAD, tk), lambda b_, h, i: (b_, 0, h, 0)),
            pl.BlockSpec((1, tq, V_HEAD), lambda b_, h, i: (b_, i, h)),
        ],
        out_specs=pl.BlockSpec((1, tq, V_HEAD), lambda b_, h, i: (b_, i, h)),
        scratch_shapes=[pltpu.VMEM((tk, tq), F32), pltpu.VMEM((tk, tq), F32),
                        pltpu.VMEM((1, tq), F32), pltpu.VMEM((1, tq), F32),
                        pltpu.VMEM((V_HEAD, tq), F32)],
        compiler_params=_params("parallel", "parallel", "arbitrary"),
        name="mla_attention",
    )(q, k, vt, gz)


def _vt_kernel(w_ref, c_ref, o_ref):
    o_ref[0, 0] = lax.dot_general(w_ref[...], c_ref[0], (((1,), (1,)), ((), ())),
                                  preferred_element_type=F32).astype(o_ref.dtype)


def _values_transposed(c_n, w_vt, *, tk):
    b, s, _ = c_n.shape
    return pl.pallas_call(
        _vt_kernel,
        out_shape=jax.ShapeDtypeStruct((b, s // tk, MLA_WIDTH, tk), BF16),
        grid=(b, s // tk),
        in_specs=[
            pl.BlockSpec((MLA_WIDTH, KV_LORA), lambda b_, j: (0, 0)),
            pl.BlockSpec((1, tk, KV_LORA), lambda b_, j: (b_, j, 1)),
        ],
        out_specs=pl.BlockSpec((1, 1, MLA_WIDTH, tk), lambda b_, j: (b_, j, 0, 0)),
        compiler_params=_params("parallel", "parallel"),
        name="mla_vt",
    )(w_vt, c_n)


def _out_ln_kernel(a_ref, x_ref, w_ref, lw_ref, lb_ref, o_ref, *maybe_ob_ref):
    y = jnp.dot(a_ref[...], w_ref[...], preferred_element_type=F32)
    t = ALPHA * x_ref[...] + y
    mu = jnp.mean(t, axis=-1, keepdims=True)
    d = t - mu
    var = jnp.mean(d * d, axis=-1, keepdims=True)
    out = d * lax.rsqrt(var + EPS) * lw_ref[...] + lb_ref[...]
    o_ref[...] = out
    for ob_ref in maybe_ob_ref:
        ob_ref[...] = out.astype(BF16)


def _out_ln(a, x, w, lw, lb, *, tm, with_bf16):
    m = x.shape[0]
    tm = min(tm, m)
    row_spec = pl.BlockSpec((tm, D_MODEL), lambda i: (i, 0))
    out_shape = [jax.ShapeDtypeStruct((m, D_MODEL), F32)]
    if with_bf16:
        out_shape.append(jax.ShapeDtypeStruct((m, D_MODEL), BF16))
    return pl.pallas_call(
        _out_ln_kernel,
        out_shape=out_shape,
        grid=(m // tm,),
        in_specs=[
            pl.BlockSpec((tm, a.shape[1]), lambda i: (i, 0)),
            row_spec,
            pl.BlockSpec(w.shape, lambda i: (0, 0)),
            pl.BlockSpec((1, D_MODEL), lambda i: (0, 0)),
            pl.BlockSpec((1, D_MODEL), lambda i: (0, 0)),
        ],
        out_specs=[row_spec] * len(out_shape),
        compiler_params=_params("parallel"),
        name="out_proj_layernorm",
    )(a, x, w, lw, lb)


def _log_sigmoid(x):
    return jnp.minimum(x, 0.0) - jnp.log1p(jnp.exp(-jnp.abs(x)))


def _cell_kernel(raw_ref, v_ref, g_ref, og_ref, cw_ref, cb_ref, hn_ref, o_ref,
                 xpad_ref, c_ref, n_ref, m_ref, *, cl):
    @pl.when(pl.program_id(1) == 0)
    def _():
        xpad_ref[0:SUBLANES, :] = jnp.zeros((SUBLANES, xpad_ref.shape[1]), F32)
        c_ref[...] = jnp.zeros(c_ref.shape, F32)
        n_ref[...] = jnp.zeros(n_ref.shape, F32)
        m_ref[...] = jnp.zeros(m_ref.shape, F32)

    xpad_ref[SUBLANES:SUBLANES + cl, :] = raw_ref[0]

    g = g_ref[0]
    rows = lax.broadcasted_iota(jnp.int32, (cl, cl), 0)
    cols = lax.broadcasted_iota(jnp.int32, (cl, cl), 1)
    tril = cols <= rows
    bcum = jnp.dot(tril.astype(F32), _log_sigmoid(g), preferred_element_type=F32,
                   precision=lax.Precision.HIGHEST)
    g_t = g.T
    bcum_t = bcum.T

    for h in range(ML_HEADS):
        def conv_silu(c0):
            acc = cb_ref[:, c0:c0 + ML_QK]
            for tap in range(CONV_W):
                off = SUBLANES - (CONV_W - 1) + tap
                acc = acc + xpad_ref[off:off + cl, c0:c0 + ML_QK] * cw_ref[tap:tap + 1, c0:c0 + ML_QK]
            return _silu(acc)

        q = conv_silu(h * ML_QK)
        k = conv_silu(ML_QK_WIDTH + h * ML_QK) * (ML_QK ** -0.5)
        v = v_ref[0, :, h * ML_V:(h + 1) * ML_V]
        qb = q.astype(BF16)

        li_row = g_t[h:h + 1, :]
        li_col = g[:, h:h + 1]
        b_row = bcum_t[ML_HEADS + h:ML_HEADS + h + 1, :]
        b_col = bcum[:, ML_HEADS + h:ML_HEADS + h + 1]
        m_st = m_ref[h][:, 0:1]
        n_st = n_ref[h]
        c_st = c_ref[h]

        dmat = jnp.where(tril, b_col - b_row + li_row, NEG_INF)
        m_inter = b_col + m_st
        m_t = jnp.maximum(m_inter, jnp.max(dmat, axis=-1, keepdims=True))
        w_inter = jnp.exp(m_inter - m_t)
        s_qk = lax.dot_general(qb, k.astype(BF16), (((1,), (1,)), ((), ())),
                               preferred_element_type=F32) * jnp.exp(dmat - m_t)
        num = (w_inter * jnp.dot(qb, c_st.astype(BF16), preferred_element_type=F32)
               + jnp.dot(s_qk.astype(BF16), v, preferred_element_type=F32))
        den = (w_inter * jnp.sum(q * n_st, axis=-1, keepdims=True)
               + jnp.sum(s_qk, axis=-1, keepdims=True))
        h_out = num / jnp.maximum(jnp.abs(den), jnp.exp(-m_t))

        b_last = b_col[cl - 1:cl, :]
        g_col = b_last - b_col + li_col
        m_new = jnp.maximum(b_last + m_st, jnp.max(g_col, axis=0, keepdims=True))
        decay = jnp.exp(b_last + m_st - m_new)
        kw = k * jnp.exp(g_col - m_new)
        c_ref[h] = decay * c_st + jnp.dot(kw.T.astype(BF16), v, preferred_element_type=F32)
        n_ref[h] = decay * n_st + jnp.sum(kw, axis=0, keepdims=True)
        m_ref[h] = jnp.broadcast_to(m_new, (1, LANES))

        hn = h_out * lax.rsqrt(jnp.mean(h_out * h_out, axis=-1, keepdims=True) + EPS)
        hn = hn * hn_ref[:, h * ML_V:(h + 1) * ML_V]
        gate = og_ref[0, :, h * ML_V:(h + 1) * ML_V].astype(F32)
        o_ref[0, :, h * ML_V:(h + 1) * ML_V] = (hn * gate).astype(o_ref.dtype)

    xpad_ref[0:SUBLANES, :] = xpad_ref[cl:cl + SUBLANES, :]


def _mlstm_cell(raw, v, gates, og, conv_w, conv_b, head_norm, *, cl):
    b, s, _ = raw.shape
    cl = min(cl, s)
    return pl.pallas_call(
        functools.partial(_cell_kernel, cl=cl),
        out_shape=jax.ShapeDtypeStruct((b, s, ML_WIDTH), BF16),
        grid=(b, s // cl),
        in_specs=[
            pl.BlockSpec((1, cl, 2 * ML_QK_WIDTH), lambda b_, c: (b_, c, 0)),
            pl.BlockSpec((1, cl, ML_WIDTH), lambda b_, c: (b_, c, 0)),
            pl.BlockSpec((1, cl, GATE_PAD), lambda b_, c: (b_, c, 0)),
            pl.BlockSpec((1, cl, ML_WIDTH), lambda b_, c: (b_, c, 0)),
            pl.BlockSpec((CONV_W, 2 * ML_QK_WIDTH), lambda b_, c: (0, 0)),
            pl.BlockSpec((1, 2 * ML_QK_WIDTH), lambda b_, c: (0, 0)),
            pl.BlockSpec((1, ML_WIDTH), lambda b_, c: (0, 0)),
        ],
        out_specs=pl.BlockSpec((1, cl, ML_WIDTH), lambda b_, c: (b_, c, 0)),
        scratch_shapes=[
            pltpu.VMEM((cl + SUBLANES, 2 * ML_QK_WIDTH), F32),
            pltpu.VMEM((ML_HEADS, ML_QK, ML_V), F32),
            pltpu.VMEM((ML_HEADS, 1, ML_QK), F32),
            pltpu.VMEM((ML_HEADS, 1, LANES), F32),
        ],
        compiler_params=_params("parallel", "arbitrary"),
        name="mlstm_cell",
    )(raw, v, gates, og, conv_w, conv_b, head_norm)


def _rope_table(positions):
    half = QK_ROPE // 2
    inv_freq = ROPE_THETA ** (-jnp.arange(0, QK_ROPE, 2, dtype=F32) / QK_ROPE)
    ang = positions.astype(F32)[..., None] * inv_freq
    cos, sin = jnp.cos(ang), jnp.sin(ang)
    z = jnp.zeros_like(cos)
    tab = jnp.concatenate([cos, cos, z, z, -sin, z, z, z, z, sin, z, z], axis=-1)
    return tab.reshape(-1, 12 * half)


def _mla_layer(x, xb, rope_tab, bsz, seq, w_in, q_norm, w_qb, kv_norm, w_kvb, w_out, lw, lb, last):
    o_kv = Q_LORA + KV_LORA
    o_z = o_kv + QK_ROPE
    w_c = w_in[:, :o_kv].astype(BF16)
    w_kr = jnp.pad(w_in[:, o_kv:o_z], ((0, 0), (0, LANES - QK_ROPE))).astype(BF16)
    w_z = w_in[:, o_z:].astype(BF16)
    norms = jnp.concatenate([q_norm, kv_norm]).reshape(1, o_kv)
    w_q = jnp.pad(w_qb.reshape(Q_LORA, MLA_HEADS, MLA_QK),
                  ((0, 0), (0, 0), (0, QK_PAD - MLA_QK))).reshape(Q_LORA, MLA_HEADS * QK_PAD).astype(BF16)
    w_kv3 = w_kvb.reshape(KV_LORA, MLA_HEADS, QK_NOPE + V_HEAD)
    w_k = w_kv3[:, :, :QK_NOPE].reshape(KV_LORA, MLA_HEADS * QK_NOPE).astype(BF16)
    w_vt = w_kv3[:, :, QK_NOPE:].reshape(KV_LORA, MLA_WIDTH).T.astype(BF16)
    tq, tk = min(ATTN_TQ, seq), min(ATTN_TK, seq // 2)

    tm_w, tn_w = _wide_tiles(xb)
    c_n, kr = _proj(xb, [w_c], _ep_rmsnorm, BF16, tm=tm_w, tn=Q_LORA, col_aux=[norms], row_aux=[rope_tab],
                    side=(w_kr, jnp.zeros((1, LANES), F32), _ep_krope, BF16), name="mla_latent")
    gz = _proj(xb, [w_z], _ep_silu, BF16, tm=tm_w, tn=tn_w, name="mla_gate")
    q = _proj(c_n, [w_q], _ep_q, BF16, tm=2048, tn=1024, xcol=0, xk=Q_LORA, row_aux=[rope_tab],
              name="mla_q")
    k = _proj(c_n, [w_k], _ep_k, BF16, tm=2048, tn=1024, out_tn=2048, xcol=1, xk=KV_LORA,
              row_aux=[kr], name="mla_k")

    shape3 = lambda t: t.reshape(bsz, seq, t.shape[-1])
    vt = _values_transposed(shape3(c_n), w_vt, tk=tk)
    a = _attention(shape3(q), shape3(k), vt, shape3(gz), tq=tq)
    return _out_ln(a.reshape(bsz * seq, MLA_WIDTH), x, w_out.astype(BF16),
                   lw.reshape(1, -1), lb.reshape(1, -1), tm=512, with_bf16=not last)


def _mlstm_layer(x, xb, bsz, seq, w_in, conv_w, conv_b, gate_b, head_norm, w_out, lw, lb, last):
    o1 = 2 * ML_QK_WIDTH
    o2 = o1 + ML_WIDTH
    o3 = o2 + ML_WIDTH
    o4 = o3 + 2 * ML_HEADS
    w_qk = w_in[:, :o1].astype(BF16)
    w_v = w_in[:, o1:o2].astype(BF16)
    w_o = w_in[:, o2:o3].astype(BF16)
    w_g = jnp.pad(w_in[:, o3:o4], ((0, 0), (0, GATE_PAD - 2 * ML_HEADS))).astype(BF16)
    w_z = w_in[:, o4:].astype(BF16)
    gb = jnp.pad(gate_b, (0, GATE_PAD - 2 * ML_HEADS)).reshape(1, GATE_PAD)

    tm_w, tn_w = _wide_tiles(xb)
    raw = _proj(xb, [w_qk], _ep_identity, F32, tm=1024, tn=tn_w, name="ml_qk")
    v, gates = _proj(xb, [w_v], _ep_identity, BF16, tm=tm_w, tn=tn_w,
                     side=(w_g, gb, _ep_bias, F32), name="ml_v")
    og = _proj(xb, [w_o, w_z], _ep_outgate, BF16, tm=1024, tn=1024, name="ml_outgate")

    shape3 = lambda t: t.reshape(bsz, seq, t.shape[-1])
    cell = _mlstm_cell(shape3(raw), shape3(v), shape3(gates), shape3(og), conv_w,
                       conv_b.reshape(1, -1), head_norm.reshape(1, -1), cl=256)
    return _out_ln(cell.reshape(bsz * seq, ML_WIDTH), x, w_out.astype(BF16),
                   lw.reshape(1, -1), lb.reshape(1, -1), tm=512, with_bf16=not last)


def kernel(x, positions, mla_w_in, mla_q_norm, mla_w_qb, mla_kv_norm, mla_w_kvb, mla_w_out,
           ml_w_in, ml_conv_w, ml_conv_b, ml_gate_b, ml_head_norm, ml_w_out, ln_w, ln_b):
    bsz, seq, _ = x.shape
    rope_tab = _rope_table(positions)
    h = x.reshape(bsz * seq, D_MODEL)
    hb = h
    for layer in range(DEPTH):
        j = layer // N_MIXERS
        last = layer == DEPTH - 1
        if layer % N_MIXERS == 0:
            outs = _mla_layer(h, hb, rope_tab, bsz, seq, mla_w_in[j], mla_q_norm[j], mla_w_qb[j],
                              mla_kv_norm[j], mla_w_kvb[j], mla_w_out[j], ln_w[layer], ln_b[layer], last)
        else:
            outs = _mlstm_layer(h, hb, bsz, seq, ml_w_in[j], ml_conv_w[j], ml_conv_b[j], ml_gate_b[j],
                                ml_head_norm[j], ml_w_out[j], ln_w[layer], ln_b[layer], last)
        h, hb = outs[0], outs[-1]
    return h.reshape(bsz, seq, D_MODEL)
```

```python
import functools

import jax
import jax.numpy as jnp
from jax import lax
from jax.experimental import pallas as pl
from jax.experimental.pallas import tpu as pltpu

D_MODEL = 2048
DEPTH = 4
CHUNK = 64
N_MIXERS = 2

MLA_HEADS = 16
QK_NOPE = 128
QK_ROPE = 64
V_HEAD = 128
Q_LORA = 512
KV_LORA = 512
ROPE_THETA = 10000.0
MLA_QK = QK_NOPE + QK_ROPE
MLA_WIDTH = MLA_HEADS * V_HEAD
QK_PAD = 256
ML_HEADS = 8
ML_QK = 128
ML_V = 256
ML_QK_WIDTH = ML_HEADS * ML_QK
ML_WIDTH = ML_HEADS * ML_V
CONV_W = 4
GATE_PAD = 128

ALPHA = (2 * DEPTH) ** 0.25
EPS = 1e-6

LANES = 128
SUBLANES = 8
VMEM_LIMIT = 56 * 1024 * 1024

F32 = jnp.float32
BF16 = jnp.bfloat16
NEG_INF = float("-inf")
LOG2E = 1.4426950408889634

ATTN_TQ = 2048
ATTN_TK = 1024
ATTN_STRIP = 256


def _params(*sem):
    return pltpu.CompilerParams(dimension_semantics=sem, vmem_limit_bytes=VMEM_LIMIT)


def _wide_tiles(x):
    return (2048, 2048) if x.dtype == BF16 else (1024, 1024)


def _silu(x):
    return x * jax.nn.sigmoid(x)


def _rope_lanes(v, tab):
    half = QK_ROPE // 2
    c = tab[:, 0:LANES]
    sa = tab[:, LANES:2 * LANES]
    sb = tab[:, 2 * LANES:3 * LANES]
    return v * c + pltpu.roll(v, LANES - half, 1) * sa + pltpu.roll(v, half, 1) * sb


def _proj_kernel(*refs, n_w, n_col, n_row, epilogue, cast_x, side_epilogue):
    x_ref = refs[0]
    w_refs = refs[1:1 + n_w]
    col_refs = refs[1 + n_w:1 + n_w + n_col]
    row_refs = refs[1 + n_w + n_col:1 + n_w + n_col + n_row]
    pos = 1 + n_w + n_col + n_row
    if side_epilogue is not None:
        sw_ref, sc_ref = refs[pos:pos + 2]
        pos += 2
    o_ref = refs[pos]
    first = pl.program_id(1) == 0
    if cast_x:
        xb_ref = refs[-1]

        @pl.when(first)
        def _():
            xb_ref[...] = x_ref[...].astype(BF16)

        xb = xb_ref[...]
    else:
        xb = x_ref[...]
    rows = [r[...] for r in row_refs]
    accs = [jnp.dot(xb, w[...], preferred_element_type=F32) for w in w_refs]
    o_ref[...] = epilogue(accs, [c[...] for c in col_refs], rows).astype(o_ref.dtype)
    if side_epilogue is not None:
        so_ref = refs[pos + 1]

        @pl.when(first)
        def _():
            acc = jnp.dot(xb, sw_ref[...], preferred_element_type=F32)
            so_ref[...] = side_epilogue([acc], [sc_ref[...]], rows).astype(so_ref.dtype)


def _proj(x, ws, epilogue, out_dtype, *, tm, tn, out_tn=None, xcol=0, xk=None,
          col_aux=(), row_aux=(), side=None, name):
    m = x.shape[0]
    xk = x.shape[1] if xk is None else xk
    n = ws[0].shape[1]
    tm = min(tm, m)
    tn = min(tn, n)
    out_tn = tn if out_tn is None else out_tn
    nj = n // tn
    cast_x = x.dtype != BF16
    in_specs = [pl.BlockSpec((tm, xk), lambda i, j: (i, xcol))]
    in_specs += [pl.BlockSpec((xk, tn), lambda i, j: (0, j)) for _ in ws]
    in_specs += [pl.BlockSpec((1, tn), lambda i, j: (0, j)) for _ in col_aux]
    in_specs += [pl.BlockSpec((tm, r.shape[1]), lambda i, j: (i, 0)) for r in row_aux]
    out_shape = jax.ShapeDtypeStruct((m, nj * out_tn), out_dtype)
    out_specs = pl.BlockSpec((tm, out_tn), lambda i, j: (i, j))
    side_args = ()
    if side is not None:
        w_side, col_side, side_epilogue, side_dtype = side
        ns = w_side.shape[1]
        in_specs += [pl.BlockSpec((xk, ns), lambda i, j: (0, 0)), pl.BlockSpec((1, ns), lambda i, j: (0, 0))]
        out_shape = (out_shape, jax.ShapeDtypeStruct((m, ns), side_dtype))
        out_specs = (out_specs, pl.BlockSpec((tm, ns), lambda i, j: (i, 0)))
        side_args = (w_side, col_side)
    kern = functools.partial(_proj_kernel, n_w=len(ws), n_col=len(col_aux), n_row=len(row_aux),
                             epilogue=epilogue, cast_x=cast_x,
                             side_epilogue=None if side is None else side[2])
    return pl.pallas_call(
        kern,
        out_shape=out_shape,
        grid=(m // tm, nj),
        in_specs=in_specs,
        out_specs=out_specs,
        scratch_shapes=[pltpu.VMEM((tm, xk), BF16)] if cast_x else [],
        compiler_params=_params("parallel", "arbitrary"),
        name=name,
    )(x, *ws, *col_aux, *row_aux, *side_args)


def _ep_identity(accs, cols, rows):
    return accs[0]


def _ep_silu(accs, cols, rows):
    return _silu(accs[0])


def _ep_rmsnorm(accs, cols, rows):
    a = accs[0]
    return a * lax.rsqrt(jnp.mean(a * a, axis=-1, keepdims=True) + EPS) * cols[0]


def _ep_krope(accs, cols, rows):
    return _rope_lanes(accs[0], rows[0])


def _ep_q(accs, cols, rows):
    a = accs[0]
    scale = MLA_QK ** -0.5 * LOG2E
    parts = []
    for h in range(a.shape[1] // QK_PAD):
        nope = a[:, h * QK_PAD:h * QK_PAD + QK_NOPE]
        rp = a[:, h * QK_PAD + QK_NOPE:(h + 1) * QK_PAD]
        parts += [nope * scale, _rope_lanes(rp, rows[0]) * scale]
    return jnp.concatenate(parts, axis=1)


def _ep_k(accs, cols, rows):
    a = accs[0]
    kr = rows[0].astype(F32)
    parts = []
    for h in range(a.shape[1] // QK_NOPE):
        parts += [a[:, h * QK_NOPE:(h + 1) * QK_NOPE], kr]
    return jnp.concatenate(parts, axis=1)


def _ep_outgate(accs, cols, rows):
    return jax.nn.sigmoid(accs[0]) * _silu(accs[1])


def _ep_bias(accs, cols, rows):
    return accs[0] + cols[0]


def _attn_kernel(q_ref, k_ref, vt_ref, g_ref, o_ref, s0_ref, s1_ref, mb0_ref, mb1_ref, m_ref, l_ref, acc_ref,
                 *, tq, tk):
    i = pl.program_id(2)
    m_ref[...] = jnp.full(m_ref.shape, NEG_INF, F32)
    l_ref[...] = jnp.zeros(l_ref.shape, F32)
    acc_ref[...] = jnp.zeros(acc_ref.shape, F32)

    def visible_rows(c, c0):
        return min(tk, c - c0 + ATTN_STRIP)

    def scores(j, s_ref, mb_ref, c0=0, diag=False):
        start = pl.multiple_of(j * tk, tk)
        for c in range(c0, tq, ATTN_STRIP):
            rows = visible_rows(c, c0) if diag else tk
            s = lax.dot_general(k_ref[0, pl.ds(start, rows), :], q_ref[0, c:c + ATTN_STRIP, :],
                                (((1,), (1,)), ((), ())), preferred_element_type=F32)
            s_ref[0:rows, c:c + ATTN_STRIP] = s
            if rows == tk:
                mb_ref[:, c:c + ATTN_STRIP] = jnp.max(s, axis=0, keepdims=True)

    def update(j, s_ref, mb_ref, masked, c0=0):
        for c in range(c0, tq, ATTN_STRIP):
            cs = slice(c, c + ATTN_STRIP)
            rows = visible_rows(c, c0) if masked else tk
            s = s_ref[0:rows, cs]
            if masked and c < c0 + tk:
                r = lax.broadcasted_iota(jnp.int32, s.shape, 0) // CHUNK
                cc = (lax.broadcasted_iota(jnp.int32, s.shape, 1) + (c - c0)) // CHUNK
                s = jnp.where(r <= cc, s, NEG_INF)
                m_blk = jnp.max(s, axis=0, keepdims=True)
            else:
                m_blk = mb_ref[:, cs]
            m_prev = m_ref[:, cs]
            m_new = jnp.maximum(m_prev, m_blk)
            alpha = jnp.exp2(m_prev - m_new)
            p = jnp.exp2(s - m_new)
            l_ref[:, cs] = alpha * l_ref[:, cs] + jnp.sum(p, axis=0, keepdims=True)
            acc_ref[:, cs] = alpha * acc_ref[:, cs] + jnp.dot(vt_ref[0, j, :, 0:rows], p.astype(BF16),
                                                              preferred_element_type=F32)
            m_ref[:, cs] = m_new

    nsub = tq // tk
    bufs = ((s0_ref, mb0_ref), (s1_ref, mb1_ref))
    scores(0, *bufs[0])

    def body(jj, carry):
        for d in range(nsub):
            scores(nsub * jj + d + 1, *bufs[(d + 1) % 2])
            update(nsub * jj + d, *bufs[d % 2], False)
        return carry

    lax.fori_loop(0, i, body, 0)
    for d in range(nsub):
        if d + 1 < nsub:
            scores(nsub * i + d + 1, *bufs[(d + 1) % 2], (d + 1) * tk, True)
        update(nsub * i + d, *bufs[d % 2], True, d * tk)
    o = (acc_ref[...] / l_ref[...]).T
    o_ref[0] = (o * g_ref[0].astype(F32)).astype(o_ref.dtype)


def _attention(q, k, vt, gz, *, tq):
    b, s, _ = q.shape
    tk = vt.shape[-1]
    assert tq % (2 * tk) == 0 and s % tq == 0
    return pl.pallas_call(
        functools.partial(_attn_kernel, tq=tq, tk=tk),
        out_shape=jax.ShapeDtypeStruct((b, s, MLA_WIDTH), BF16),
        grid=(b, MLA_HEADS, s // tq),
        in_specs=[
            pl.BlockSpec((1, tq, QK_PAD), lambda b_, h, i: (b_, i, h)),
            pl.BlockSpec((1, s, QK_PAD), lambda b_, h, i: (b_, 0, h)),
            pl.BlockSpec((1, s // tk, V_HEAD, tk), lambda b_, h, i: (b_, 0, h, 0)),
            pl.BlockSpec((1, tq, V_HEAD), lambda b_, h, i: (b_, i, h)),
        ],
        out_specs=pl.BlockSpec((1, tq, V_HEAD), lambda b_, h, i: (b_, i, h)),
        scratch_shapes=[pltpu.VMEM((tk, tq), F32), pltpu.VMEM((tk, tq), F32),
                        pltpu.VMEM((1, tq), F32), pltpu.VMEM((1, tq), F32),
                        pltpu.VMEM((1, tq), F32), pltpu.VMEM((1, tq), F32),
                        pltpu.VMEM((V_HEAD, tq), F32)],
        compiler_params=_params("parallel", "parallel", "arbitrary"),
        name="mla_attention",
    )(q, k, vt, gz)


def _vt_kernel(w_ref, c_ref, o_ref):
    o_ref[0, 0] = lax.dot_general(w_ref[...], c_ref[0], (((1,), (1,)), ((), ())),
                                  preferred_element_type=F32).astype(o_ref.dtype)


def _values_transposed(c_n, w_vt, *, tk):
    b, s, _ = c_n.shape
    return pl.pallas_call(
        _vt_kernel,
        out_shape=jax.ShapeDtypeStruct((b, s // tk, MLA_WIDTH, tk), BF16),
        grid=(b, s // tk),
        in_specs=[
            pl.BlockSpec((MLA_WIDTH, KV_LORA), lambda b_, j: (0, 0)),
            pl.BlockSpec((1, tk, KV_LORA), lambda b_, j: (b_, j, 1)),
        ],
        out_specs=pl.BlockSpec((1, 1, MLA_WIDTH, tk), lambda b_, j: (b_, j, 0, 0)),
        compiler_params=_params("parallel", "parallel"),
        name="mla_vt",
    )(w_vt, c_n)


def _out_ln_kernel(a_ref, x_ref, w_ref, lw_ref, lb_ref, o_ref, *maybe_ob_ref):
    y = jnp.dot(a_ref[...], w_ref[...], preferred_element_type=F32)
    t = ALPHA * x_ref[...] + y
    mu = jnp.mean(t, axis=-1, keepdims=True)
    d = t - mu
    var = jnp.mean(d * d, axis=-1, keepdims=True)
    out = d * lax.rsqrt(var + EPS) * lw_ref[...] + lb_ref[...]
    o_ref[...] = out
    for ob_ref in maybe_ob_ref:
        ob_ref[...] = out.astype(BF16)


def _out_ln(a, x, w, lw, lb, *, tm, with_bf16):
    m = x.shape[0]
    tm = min(tm, m)
    row_spec = pl.BlockSpec((tm, D_MODEL), lambda i: (i, 0))
    out_shape = [jax.ShapeDtypeStruct((m, D_MODEL), F32)]
    if with_bf16:
        out_shape.append(jax.ShapeDtypeStruct((m, D_MODEL), BF16))
    return pl.pallas_call(
        _out_ln_kernel,
        out_shape=out_shape,
        grid=(m // tm,),
        in_specs=[
            pl.BlockSpec((tm, a.shape[1]), lambda i: (i, 0)),
            row_spec,
            pl.BlockSpec(w.shape, lambda i: (0, 0)),
            pl.BlockSpec((1, D_MODEL), lambda i: (0, 0)),
            pl.BlockSpec((1, D_MODEL), lambda i: (0, 0)),
        ],
        out_specs=[row_spec] * len(out_shape),
        compiler_params=_params("parallel"),
        name="out_proj_layernorm",
    )(a, x, w, lw, lb)


def _log_sigmoid(x):
    return jnp.minimum(x, 0.0) - jnp.log1p(jnp.exp(-jnp.abs(x)))


def _cell_kernel(raw_ref, v_ref, g_ref, og_ref, cw_ref, cb_ref, hn_ref, o_ref,
                 xpad_ref, c_ref, n_ref, m_ref, *, cl):
    @pl.when(pl.program_id(1) == 0)
    def _():
        xpad_ref[0:SUBLANES, :] = jnp.zeros((SUBLANES, xpad_ref.shape[1]), F32)
        c_ref[...] = jnp.zeros(c_ref.shape, F32)
        n_ref[...] = jnp.zeros(n_ref.shape, F32)
        m_ref[...] = jnp.zeros(m_ref.shape, F32)

    xpad_ref[SUBLANES:SUBLANES + cl, :] = raw_ref[0]

    g = g_ref[0]
    rows = lax.broadcasted_iota(jnp.int32, (cl, cl), 0)
    cols = lax.broadcasted_iota(jnp.int32, (cl, cl), 1)
    tril = cols <= rows
    bcum = jnp.dot(tril.astype(F32), _log_sigmoid(g), preferred_element_type=F32,
                   precision=lax.Precision.HIGHEST)
    g_t = g.T
    bcum_t = bcum.T

    for h in range(ML_HEADS):
        def conv_silu(c0):
            acc = cb_ref[:, c0:c0 + ML_QK]
            for tap in range(CONV_W):
                off = SUBLANES - (CONV_W - 1) + tap
                acc = acc + xpad_ref[off:off + cl, c0:c0 + ML_QK] * cw_ref[tap:tap + 1, c0:c0 + ML_QK]
            return _silu(acc)

        q = conv_silu(h * ML_QK)
        k = conv_silu(ML_QK_WIDTH + h * ML_QK) * (ML_QK ** -0.5)
        v = v_ref[0, :, h * ML_V:(h + 1) * ML_V]
        qb = q.astype(BF16)

        li_row = g_t[h:h + 1, :]
        li_col = g[:, h:h + 1]
        b_row = bcum_t[ML_HEADS + h:ML_HEADS + h + 1, :]
        b_col = bcum[:, ML_HEADS + h:ML_HEADS + h + 1]
        m_st = m_ref[h][:, 0:1]
        n_st = n_ref[h]
        c_st = c_ref[h]

        dmat = jnp.where(tril, b_col - b_row + li_row, NEG_INF)
        m_inter = b_col + m_st
        m_t = jnp.maximum(m_inter, jnp.max(dmat, axis=-1, keepdims=True))
        w_inter = jnp.exp(m_inter - m_t)
        s_qk = lax.dot_general(qb, k.astype(BF16), (((1,), (1,)), ((), ())),
                               preferred_element_type=F32) * jnp.exp(dmat - m_t)
        num = (w_inter * jnp.dot(qb, c_st.astype(BF16), preferred_element_type=F32)
               + jnp.dot(s_qk.astype(BF16), v, preferred_element_type=F32))
        den = (w_inter * jnp.sum(q * n_st, axis=-1, keepdims=True)
               + jnp.sum(s_qk, axis=-1, keepdims=True))
        h_out = num / jnp.maximum(jnp.abs(den), jnp.exp(-m_t))

        b_last = b_col[cl - 1:cl, :]
        g_col = b_last - b_col + li_col
        m_new = jnp.maximum(b_last + m_st, jnp.max(g_col, axis=0, keepdims=True))
        decay = jnp.exp(b_last + m_st - m_new)
        kw = k * jnp.exp(g_col - m_new)
        c_ref[h] = decay * c_st + jnp.dot(kw.T.astype(BF16), v, preferred_element_type=F32)
        n_ref[h] = decay * n_st + jnp.sum(kw, axis=0, keepdims=True)
        m_ref[h] = jnp.broadcast_to(m_new, (1, LANES))

        hn = h_out * lax.rsqrt(jnp.mean(h_out * h_out, axis=-1, keepdims=True) + EPS)
        hn = hn * hn_ref[:, h * ML_V:(h + 1) * ML_V]
        gate = og_ref[0, :, h * ML_V:(h + 1) * ML_V].astype(F32)
        o_ref[0, :, h * ML_V:(h + 1) * ML_V] = (hn * gate).astype(o_ref.dtype)

    xpad_ref[0:SUBLANES, :] = xpad_ref[cl:cl + SUBLANES, :]


def _mlstm_cell(raw, v, gates, og, conv_w, conv_b, head_norm, *, cl):
    b, s, _ = raw.shape
    cl = min(cl, s)
    return pl.pallas_call(
        functools.partial(_cell_kernel, cl=cl),
        out_shape=jax.ShapeDtypeStruct((b, s, ML_WIDTH), BF16),
        grid=(b, s // cl),
        in_specs=[
            pl.BlockSpec((1, cl, 2 * ML_QK_WIDTH), lambda b_, c: (b_, c, 0)),
            pl.BlockSpec((1, cl, ML_WIDTH), lambda b_, c: (b_, c, 0)),
            pl.BlockSpec((1, cl, GATE_PAD), lambda b_, c: (b_, c, 0)),
            pl.BlockSpec((1, cl, ML_WIDTH), lambda b_, c: (b_, c, 0)),
            pl.BlockSpec((CONV_W, 2 * ML_QK_WIDTH), lambda b_, c: (0, 0)),
            pl.BlockSpec((1, 2 * ML_QK_WIDTH), lambda b_, c: (0, 0)),
            pl.BlockSpec((1, ML_WIDTH), lambda b_, c: (0, 0)),
        ],
        out_specs=pl.BlockSpec((1, cl, ML_WIDTH), lambda b_, c: (b_, c, 0)),
        scratch_shapes=[
            pltpu.VMEM((cl + SUBLANES, 2 * ML_QK_WIDTH), F32),
            pltpu.VMEM((ML_HEADS, ML_QK, ML_V), F32),
            pltpu.VMEM((ML_HEADS, 1, ML_QK), F32),
            pltpu.VMEM((ML_HEADS, 1, LANES), F32),
        ],
        compiler_params=_params("parallel", "arbitrary"),
        name="mlstm_cell",
    )(raw, v, gates, og, conv_w, conv_b, head_norm)


def _rope_table(positions):
    half = QK_ROPE // 2
    inv_freq = ROPE_THETA ** (-jnp.arange(0, QK_ROPE, 2, dtype=F32) / QK_ROPE)
    ang = positions.astype(F32)[..., None] * inv_freq
    cos, sin = jnp.cos(ang), jnp.sin(ang)
    z = jnp.zeros_like(cos)
    tab = jnp.concatenate([cos, cos, z, z, -sin, z, z, z, z, sin, z, z], axis=-1)
    return tab.reshape(-1, 12 * half)


def _mla_layer(x, xb, rope_tab, bsz, seq, w_in, q_norm, w_qb, kv_norm, w_kvb, w_out, lw, lb, last):
    o_kv = Q_LORA + KV_LORA
    o_z = o_kv + QK_ROPE
    w_c = w_in[:, :o_kv].astype(BF16)
    w_kr = jnp.pad(w_in[:, o_kv:o_z], ((0, 0), (0, LANES - QK_ROPE))).astype(BF16)
    w_z = w_in[:, o_z:].astype(BF16)
    norms = jnp.concatenate([q_norm, kv_norm]).reshape(1, o_kv)
    w_q = jnp.pad(w_qb.reshape(Q_LORA, MLA_HEADS, MLA_QK),
                  ((0, 0), (0, 0), (0, QK_PAD - MLA_QK))).reshape(Q_LORA, MLA_HEADS * QK_PAD).astype(BF16)
    w_kv3 = w_kvb.reshape(KV_LORA, MLA_HEADS, QK_NOPE + V_HEAD)
    w_k = w_kv3[:, :, :QK_NOPE].reshape(KV_LORA, MLA_HEADS * QK_NOPE).astype(BF16)
    w_vt = w_kv3[:, :, QK_NOPE:].reshape(KV_LORA, MLA_WIDTH).T.astype(BF16)
    tq, tk = min(ATTN_TQ, seq), min(ATTN_TK, seq // 2)

    tm_w, tn_w = _wide_tiles(xb)
    c_n, kr = _proj(xb, [w_c], _ep_rmsnorm, BF16, tm=tm_w, tn=Q_LORA, col_aux=[norms], row_aux=[rope_tab],
                    side=(w_kr, jnp.zeros((1, LANES), F32), _ep_krope, BF16), name="mla_latent")
    gz = _proj(xb, [w_z], _ep_silu, BF16, tm=tm_w, tn=tn_w, name="mla_gate")
    q = _proj(c_n, [w_q], _ep_q, BF16, tm=2048, tn=1024, xcol=0, xk=Q_LORA, row_aux=[rope_tab],
              name="mla_q")
    k = _proj(c_n, [w_k], _ep_k, BF16, tm=2048, tn=1024, out_tn=2048, xcol=1, xk=KV_LORA,
              row_aux=[kr], name="mla_k")

    shape3 = lambda t: t.reshape(bsz, seq, t.shape[-1])
    vt = _values_transposed(shape3(c_n), w_vt, tk=tk)
    a = _attention(shape3(q), shape3(k), vt, shape3(gz), tq=tq)
    return _out_ln(a.reshape(bsz * seq, MLA_WIDTH), x, w_out.astype(BF16),
                   lw.reshape(1, -1), lb.reshape(1, -1), tm=512, with_bf16=not last)


def _mlstm_layer(x, xb, bsz, seq, w_in, conv_w, conv_b, gate_b, head_norm, w_out, lw, lb, last):
    o1 = 2 * ML_QK_WIDTH
    o2 = o1 + ML_WIDTH
    o3 = o2 + ML_WIDTH
    o4 = o3 + 2 * ML_HEADS
    w_qk = w_in[:, :o1].astype(BF16)
    w_v = w_in[:, o1:o2].astype(BF16)
    w_o = w_in[:, o2:o3].astype(BF16)
    w_g = jnp.pad(w_in[:, o3:o4], ((0, 0), (0, GATE_PAD - 2 * ML_HEADS))).astype(BF16)
    w_z = w_in[:, o4:].astype(BF16)
    gb = jnp.pad(gate_b, (0, GATE_PAD - 2 * ML_HEADS)).reshape(1, GATE_PAD)

    tm_w, tn_w = _wide_tiles(xb)
    raw = _proj(xb, [w_qk], _ep_identity, F32, tm=1024, tn=tn_w, name="ml_qk")
    v, gates = _proj(xb, [w_v], _ep_identity, BF16, tm=tm_w, tn=tn_w,
                     side=(w_g, gb, _ep_bias, F32), name="ml_v")
    og = _proj(xb, [w_o, w_z], _ep_outgate, BF16, tm=1024, tn=1024, name="ml_outgate")

    shape3 = lambda t: t.reshape(bsz, seq, t.shape[-1])
    cell = _mlstm_cell(shape3(raw), shape3(v), shape3(gates), shape3(og), conv_w,
                       conv_b.reshape(1, -1), head_norm.reshape(1, -1), cl=256)
    return _out_ln(cell.reshape(bsz * seq, ML_WIDTH), x, w_out.astype(BF16),
                   lw.reshape(1, -1), lb.reshape(1, -1), tm=512, with_bf16=not last)


def kernel(x, positions, mla_w_in, mla_q_norm, mla_w_qb, mla_kv_norm, mla_w_kvb, mla_w_out,
           ml_w_in, ml_conv_w, ml_conv_b, ml_gate_b, ml_head_norm, ml_w_out, ln_w, ln_b):
    bsz, seq, _ = x.shape
    rope_tab = _rope_table(positions)
    h = x.reshape(bsz * seq, D_MODEL)
    hb = h
    for layer in range(DEPTH):
        j = layer // N_MIXERS
        last = layer == DEPTH - 1
        if layer % N_MIXERS == 0:
            outs = _mla_layer(h, hb, rope_tab, bsz, seq, mla_w_in[j], mla_q_norm[j], mla_w_qb[j],
                              mla_kv_norm[j], mla_w_kvb[j], mla_w_out[j], ln_w[layer], ln_b[layer], last)
        else:
            outs = _mlstm_layer(h, hb, bsz, seq, ml_w_in[j], ml_conv_w[j], ml_conv_b[j], ml_gate_b[j],
                                ml_head_norm[j], ml_w_out[j], ln_w[layer], ln_b[layer], last)
        h, hb = outs[0], outs[-1]
    return h.reshape(bsz, seq, D_MODEL)
```

```python
import functools

import jax
import jax.numpy as jnp
from jax import lax
from jax.experimental import pallas as pl
from jax.experimental.pallas import tpu as pltpu

D_MODEL = 2048
DEPTH = 4
CHUNK = 64
N_MIXERS = 2

MLA_HEADS = 16
QK_NOPE = 128
QK_ROPE = 64
V_HEAD = 128
Q_LORA = 512
KV_LORA = 512
ROPE_THETA = 10000.0
MLA_QK = QK_NOPE + QK_ROPE
MLA_WIDTH = MLA_HEADS * V_HEAD
QK_PAD = 256
ML_HEADS = 8
ML_QK = 128
ML_V = 256
ML_QK_WIDTH = ML_HEADS * ML_QK
ML_WIDTH = ML_HEADS * ML_V
CONV_W = 4
GATE_PAD = 128

ALPHA = (2 * DEPTH) ** 0.25
EPS = 1e-6

LANES = 128
SUBLANES = 8
VMEM_LIMIT = 56 * 1024 * 1024

F32 = jnp.float32
BF16 = jnp.bfloat16
NEG_INF = float("-inf")
LOG2E = 1.4426950408889634

ATTN_TQ = 2048
ATTN_TK = 1024
ATTN_STRIP = 256


def _params(*sem):
    return pltpu.CompilerParams(dimension_semantics=sem, vmem_limit_bytes=VMEM_LIMIT)


def _wide_tiles(x):
    return (2048, 2048) if x.dtype == BF16 else (1024, 1024)


def _silu(x):
    return x * jax.nn.sigmoid(x)


def _rope_lanes(v, tab):
    half = QK_ROPE // 2
    c = tab[:, 0:LANES]
    sa = tab[:, LANES:2 * LANES]
    sb = tab[:, 2 * LANES:3 * LANES]
    return v * c + pltpu.roll(v, LANES - half, 1) * sa + pltpu.roll(v, half, 1) * sb


def _proj_kernel(*refs, n_w, n_col, n_row, epilogue, cast_x, side_epilogue):
    x_ref = refs[0]
    w_refs = refs[1:1 + n_w]
    col_refs = refs[1 + n_w:1 + n_w + n_col]
    row_refs = refs[1 + n_w + n_col:1 + n_w + n_col + n_row]
    pos = 1 + n_w + n_col + n_row
    if side_epilogue is not None:
        sw_ref, sc_ref = refs[pos:pos + 2]
        pos += 2
    o_ref = refs[pos]
    first = pl.program_id(1) == 0
    if cast_x:
        xb_ref = refs[-1]

        @pl.when(first)
        def _():
            xb_ref[...] = x_ref[...].astype(BF16)

        xb = xb_ref[...]
    else:
        xb = x_ref[...]
    rows = [r[...] for r in row_refs]
    accs = [jnp.dot(xb, w[...], preferred_element_type=F32) for w in w_refs]
    o_ref[...] = epilogue(accs, [c[...] for c in col_refs], rows).astype(o_ref.dtype)
    if side_epilogue is not None:
        so_ref = refs[pos + 1]

        @pl.when(first)
        def _():
            acc = jnp.dot(xb, sw_ref[...], preferred_element_type=F32)
            so_ref[...] = side_epilogue([acc], [sc_ref[...]], rows).astype(so_ref.dtype)


def _proj(x, ws, epilogue, out_dtype, *, tm, tn, out_tn=None, xcol=0, xk=None,
          col_aux=(), row_aux=(), side=None, name):
    m = x.shape[0]
    xk = x.shape[1] if xk is None else xk
    n = ws[0].shape[1]
    tm = min(tm, m)
    tn = min(tn, n)
    out_tn = tn if out_tn is None else out_tn
    nj = n // tn
    cast_x = x.dtype != BF16
    in_specs = [pl.BlockSpec((tm, xk), lambda i, j: (i, xcol))]
    in_specs += [pl.BlockSpec((xk, tn), lambda i, j: (0, j)) for _ in ws]
    in_specs += [pl.BlockSpec((1, tn), lambda i, j: (0, j)) for _ in col_aux]
    in_specs += [pl.BlockSpec((tm, r.shape[1]), lambda i, j: (i, 0)) for r in row_aux]
    out_shape = jax.ShapeDtypeStruct((m, nj * out_tn), out_dtype)
    out_specs = pl.BlockSpec((tm, out_tn), lambda i, j: (i, j))
    side_args = ()
    if side is not None:
        w_side, col_side, side_epilogue, side_dtype = side
        ns = w_side.shape[1]
        in_specs += [pl.BlockSpec((xk, ns), lambda i, j: (0, 0)), pl.BlockSpec((1, ns), lambda i, j: (0, 0))]
        out_shape = (out_shape, jax.ShapeDtypeStruct((m, ns), side_dtype))
        out_specs = (out_specs, pl.BlockSpec((tm, ns), lambda i, j: (i, 0)))
        side_args = (w_side, col_side)
    kern = functools.partial(_proj_kernel, n_w=len(ws), n_col=len(col_aux), n_row=len(row_aux),
                             epilogue=epilogue, cast_x=cast_x,
                             side_epilogue=None if side is None else side[2])
    return pl.pallas_call(
        kern,
        out_shape=out_shape,
        grid=(m // tm, nj),
        in_specs=in_specs,
        out_specs=out_specs,
        scratch_shapes=[pltpu.VMEM((tm, xk), BF16)] if cast_x else [],
        compiler_params=_params("parallel", "arbitrary"),
        name=name,
    )(x, *ws, *col_aux, *row_aux, *side_args)


def _ep_identity(accs, cols, rows):
    return accs[0]


def _ep_silu(accs, cols, rows):
    return _silu(accs[0])


def _ep_rmsnorm(accs, cols, rows):
    a = accs[0]
    return a * lax.rsqrt(jnp.mean(a * a, axis=-1, keepdims=True) + EPS) * cols[0]


def _ep_krope(accs, cols, rows):
    return _rope_lanes(accs[0], rows[0])


def _ep_q(accs, cols, rows):
    a = accs[0]
    scale = MLA_QK ** -0.5 * LOG2E
    parts = []
    for h in range(a.shape[1] // QK_PAD):
        nope = a[:, h * QK_PAD:h * QK_PAD + QK_NOPE]
        rp = a[:, h * QK_PAD + QK_NOPE:(h + 1) * QK_PAD]
        parts += [nope * scale, _rope_lanes(rp, rows[0]) * scale]
    return jnp.concatenate(parts, axis=1)


def _ep_k(accs, cols, rows):
    a = accs[0]
    kr = rows[0].astype(F32)
    parts = []
    for h in range(a.shape[1] // QK_NOPE):
        parts += [a[:, h * QK_NOPE:(h + 1) * QK_NOPE], kr]
    return jnp.concatenate(parts, axis=1)


def _ep_outgate(accs, cols, rows):
    return jax.nn.sigmoid(accs[0]) * _silu(accs[1])


def _ep_bias(accs, cols, rows):
    return accs[0] + cols[0]


def _attn_kernel(q_ref, k_ref, vt_ref, g_ref, o_ref, s0_ref, s1_ref, mb0_ref, mb1_ref, m_ref, l_ref, acc_ref,
                 *, tq, tk):
    i = pl.program_id(2)
    m_ref[...] = jnp.full(m_ref.shape, NEG_INF, F32)
    l_ref[...] = jnp.zeros(l_ref.shape, F32)
    acc_ref[...] = jnp.zeros(acc_ref.shape, F32)

    def visible_rows(c, c0):
        return min(tk, c - c0 + ATTN_STRIP)

    def score_strip(j, s_ref, mb_ref, c, c0=0, diag=False):
        rows = visible_rows(c, c0) if diag else tk
        cs = slice(c, c + ATTN_STRIP)
        s = lax.dot_general(k_ref[0, pl.ds(pl.multiple_of(j * tk, tk), rows), :], q_ref[0, cs, :],
                            (((1,), (1,)), ((), ())), preferred_element_type=F32)
        s_ref[0:rows, cs] = s
        if rows == tk:
            mb_ref[:, cs] = jnp.max(s, axis=0, keepdims=True)

    def update_strip(j, s_ref, mb_ref, c, c0=0, masked=False):
        cs = slice(c, c + ATTN_STRIP)
        rows = visible_rows(c, c0) if masked else tk
        s = s_ref[0:rows, cs]
        if masked and c < c0 + tk:
            r = lax.broadcasted_iota(jnp.int32, s.shape, 0) // CHUNK
            cc = (lax.broadcasted_iota(jnp.int32, s.shape, 1) + (c - c0)) // CHUNK
            s = jnp.where(r <= cc, s, NEG_INF)
            m_blk = jnp.max(s, axis=0, keepdims=True)
        else:
            m_blk = mb_ref[:, cs]
        m_prev = m_ref[:, cs]
        m_new = jnp.maximum(m_prev, m_blk)
        alpha = jnp.exp2(m_prev - m_new)
        p = jnp.exp2(s - m_new)
        l_ref[:, cs] = alpha * l_ref[:, cs] + jnp.sum(p, axis=0, keepdims=True)
        acc_ref[:, cs] = alpha * acc_ref[:, cs] + jnp.dot(vt_ref[0, j, :, 0:rows], p.astype(BF16),
                                                          preferred_element_type=F32)
        m_ref[:, cs] = m_new

    nsub = tq // tk
    bufs = ((s0_ref, mb0_ref), (s1_ref, mb1_ref))
    strips = list(range(0, tq, ATTN_STRIP))
    for c in strips:
        score_strip(0, *bufs[0], c)

    def body(jj, carry):
        for d in range(nsub):
            cur, nxt = bufs[d % 2], bufs[(d + 1) % 2]
            j = nsub * jj + d
            score_strip(j + 1, *nxt, strips[0])
            for n, c in enumerate(strips):
                if n + 1 < len(strips):
                    score_strip(j + 1, *nxt, strips[n + 1])
                update_strip(j, *cur, c)
        return carry

    lax.fori_loop(0, i, body, 0)
    for d in range(nsub):
        cur, nxt = bufs[d % 2], bufs[(d + 1) % 2]
        c0 = d * tk
        for c in range(c0, tq, ATTN_STRIP):
            if d + 1 < nsub and c >= c0 + tk:
                score_strip(nsub * i + d + 1, *nxt, c, c0 + tk, diag=True)
            update_strip(nsub * i + d, *cur, c, c0, masked=True)
    for c in strips:
        cs = slice(c, c + ATTN_STRIP)
        o = (acc_ref[:, cs] / l_ref[:, cs]).T
        o_ref[0, cs, :] = (o * g_ref[0, cs, :].astype(F32)).astype(o_ref.dtype)


def _attention(q, k, vt, gz, *, tq):
    b, s, _ = q.shape
    tk = vt.shape[-1]
    assert tq % (2 * tk) == 0 and s % tq == 0
    return pl.pallas_call(
        functools.partial(_attn_kernel, tq=tq, tk=tk),
        out_shape=jax.ShapeDtypeStruct((b, s, MLA_WIDTH), BF16),
        grid=(b, MLA_HEADS, s // tq),
        in_specs=[
            pl.BlockSpec((1, tq, QK_PAD), lambda b_, h, i: (b_, i, h)),
            pl.BlockSpec((1, s, QK_PAD), lambda b_, h, i: (b_, 0, h)),
            pl.BlockSpec((1, s // tk, V_HEAD, tk), lambda b_, h, i: (b_, 0, h, 0)),
            pl.BlockSpec((1, tq, V_HEAD), lambda b_, h, i: (b_, i, h)),
        ],
        out_specs=pl.BlockSpec((1, tq, V_HEAD), lambda b_, h, i: (b_, i, h)),
        scratch_shapes=[pltpu.VMEM((tk, tq), F32), pltpu.VMEM((tk, tq), F32),
                        pltpu.VMEM((1, tq), F32), pltpu.VMEM((1, tq), F32),
                        pltpu.VMEM((1, tq), F32), pltpu.VMEM((1, tq), F32),
                        pltpu.VMEM((V_HEAD, tq), F32)],
        compiler_params=_params("parallel", "parallel", "arbitrary"),
        name="mla_attention",
    )(q, k, vt, gz)


def _vt_kernel(w_ref, c_ref, o_ref):
    o_ref[0, 0] = lax.dot_general(w_ref[...], c_ref[0], (((1,), (1,)), ((), ())),
                                  preferred_element_type=F32).astype(o_ref.dtype)


def _values_transposed(c_n, w_vt, *, tk):
    b, s, _ = c_n.shape
    return pl.pallas_call(
        _vt_kernel,
        out_shape=jax.ShapeDtypeStruct((b, s // tk, MLA_WIDTH, tk), BF16),
        grid=(b, s // tk),
        in_specs=[
            pl.BlockSpec((MLA_WIDTH, KV_LORA), lambda b_, j: (0, 0)),
            pl.BlockSpec((1, tk, KV_LORA), lambda b_, j: (b_, j, 1)),
        ],
        out_specs=pl.BlockSpec((1, 1, MLA_WIDTH, tk), lambda b_, j: (b_, j, 0, 0)),
        compiler_params=_params("parallel", "parallel"),
        name="mla_vt",
    )(w_vt, c_n)


def _out_ln_kernel(a_ref, x_ref, w_ref, lw_ref, lb_ref, o_ref, *maybe_ob_ref):
    y = jnp.dot(a_ref[...], w_ref[...], preferred_element_type=F32)
    t = ALPHA * x_ref[...] + y
    mu = jnp.mean(t, axis=-1, keepdims=True)
    d = t - mu
    var = jnp.mean(d * d, axis=-1, keepdims=True)
    out = d * lax.rsqrt(var + EPS) * lw_ref[...] + lb_ref[...]
    o_ref[...] = out
    for ob_ref in maybe_ob_ref:
        ob_ref[...] = out.astype(BF16)


def _out_ln(a, x, w, lw, lb, *, tm, with_bf16):
    m = x.shape[0]
    tm = min(tm, m)
    row_spec = pl.BlockSpec((tm, D_MODEL), lambda i: (i, 0))
    out_shape = [jax.ShapeDtypeStruct((m, D_MODEL), F32)]
    if with_bf16:
        out_shape.append(jax.ShapeDtypeStruct((m, D_MODEL), BF16))
    return pl.pallas_call(
        _out_ln_kernel,
        out_shape=out_shape,
        grid=(m // tm,),
        in_specs=[
            pl.BlockSpec((tm, a.shape[1]), lambda i: (i, 0)),
            row_spec,
            pl.BlockSpec(w.shape, lambda i: (0, 0)),
            pl.BlockSpec((1, D_MODEL), lambda i: (0, 0)),
            pl.BlockSpec((1, D_MODEL), lambda i: (0, 0)),
        ],
        out_specs=[row_spec] * len(out_shape),
        compiler_params=_params("parallel"),
        name="out_proj_layernorm",
    )(a, x, w, lw, lb)


def _log_sigmoid(x):
    return jnp.minimum(x, 0.0) - jnp.log1p(jnp.exp(-jnp.abs(x)))


def _cell_kernel(raw_ref, v_ref, g_ref, og_ref, cw_ref, cb_ref, hn_ref, o_ref,
                 xpad_ref, c_ref, n_ref, m_ref, *, cl):
    @pl.when(pl.program_id(1) == 0)
    def _():
        xpad_ref[0:SUBLANES, :] = jnp.zeros((SUBLANES, xpad_ref.shape[1]), F32)
        c_ref[...] = jnp.zeros(c_ref.shape, F32)
        n_ref[...] = jnp.zeros(n_ref.shape, F32)
        m_ref[...] = jnp.zeros(m_ref.shape, F32)

    xpad_ref[SUBLANES:SUBLANES + cl, :] = raw_ref[0]

    g = g_ref[0]
    rows = lax.broadcasted_iota(jnp.int32, (cl, cl), 0)
    cols = lax.broadcasted_iota(jnp.int32, (cl, cl), 1)
    tril = cols <= rows
    bcum = jnp.dot(tril.astype(F32), _log_sigmoid(g), preferred_element_type=F32,
                   precision=lax.Precision.HIGHEST)
    g_t = g.T
    bcum_t = bcum.T

    for h in range(ML_HEADS):
        def conv_silu(c0):
            acc = cb_ref[:, c0:c0 + ML_QK]
            for tap in range(CONV_W):
                off = SUBLANES - (CONV_W - 1) + tap
                acc = acc + xpad_ref[off:off + cl, c0:c0 + ML_QK] * cw_ref[tap:tap + 1, c0:c0 + ML_QK]
            return _silu(acc)

        q = conv_silu(h * ML_QK)
        k = conv_silu(ML_QK_WIDTH + h * ML_QK) * (ML_QK ** -0.5)
        v = v_ref[0, :, h * ML_V:(h + 1) * ML_V]
        qb = q.astype(BF16)

        li_row = g_t[h:h + 1, :]
        li_col = g[:, h:h + 1]
        b_row = bcum_t[ML_HEADS + h:ML_HEADS + h + 1, :]
        b_col = bcum[:, ML_HEADS + h:ML_HEADS + h + 1]
        m_st = m_ref[h][:, 0:1]
        n_st = n_ref[h]
        c_st = c_ref[h]

        dmat = jnp.where(tril, b_col - b_row + li_row, NEG_INF)
        m_inter = b_col + m_st
        m_t = jnp.maximum(m_inter, jnp.max(dmat, axis=-1, keepdims=True))
        w_inter = jnp.exp(m_inter - m_t)
        s_qk = lax.dot_general(qb, k.astype(BF16), (((1,), (1,)), ((), ())),
                               preferred_element_type=F32) * jnp.exp(dmat - m_t)
        num = (w_inter * jnp.dot(qb, c_st.astype(BF16), preferred_element_type=F32)
               + jnp.dot(s_qk.astype(BF16), v, preferred_element_type=F32))
        den = (w_inter * jnp.sum(q * n_st, axis=-1, keepdims=True)
               + jnp.sum(s_qk, axis=-1, keepdims=True))
        h_out = num / jnp.maximum(jnp.abs(den), jnp.exp(-m_t))

        b_last = b_col[cl - 1:cl, :]
        g_col = b_last - b_col + li_col
        m_new = jnp.maximum(b_last + m_st, jnp.max(g_col, axis=0, keepdims=True))
        decay = jnp.exp(b_last + m_st - m_new)
        kw = k * jnp.exp(g_col - m_new)
        c_ref[h] = decay * c_st + jnp.dot(kw.T.astype(BF16), v, preferred_element_type=F32)
        n_ref[h] = decay * n_st + jnp.sum(kw, axis=0, keepdims=True)
        m_ref[h] = jnp.broadcast_to(m_new, (1, LANES))

        hn = h_out * lax.rsqrt(jnp.mean(h_out * h_out, axis=-1, keepdims=True) + EPS)
        hn = hn * hn_ref[:, h * ML_V:(h + 1) * ML_V]
        gate = og_ref[0, :, h * ML_V:(h + 1) * ML_V].astype(F32)
        o_ref[0, :, h * ML_V:(h + 1) * ML_V] = (hn * gate).astype(o_ref.dtype)

    xpad_ref[0:SUBLANES, :] = xpad_ref[cl:cl + SUBLANES, :]


def _mlstm_cell(raw, v, gates, og, conv_w, conv_b, head_norm, *, cl):
    b, s, _ = raw.shape
    cl = min(cl, s)
    return pl.pallas_call(
        functools.partial(_cell_kernel, cl=cl),
        out_shape=jax.ShapeDtypeStruct((b, s, ML_WIDTH), BF16),
        grid=(b, s // cl),
        in_specs=[
            pl.BlockSpec((1, cl, 2 * ML_QK_WIDTH), lambda b_, c: (b_, c, 0)),
            pl.BlockSpec((1, cl, ML_WIDTH), lambda b_, c: (b_, c, 0)),
            pl.BlockSpec((1, cl, GATE_PAD), lambda b_, c: (b_, c, 0)),
            pl.BlockSpec((1, cl, ML_WIDTH), lambda b_, c: (b_, c, 0)),
            pl.BlockSpec((CONV_W, 2 * ML_QK_WIDTH), lambda b_, c: (0, 0)),
            pl.BlockSpec((1, 2 * ML_QK_WIDTH), lambda b_, c: (0, 0)),
            pl.BlockSpec((1, ML_WIDTH), lambda b_, c: (0, 0)),
        ],
        out_specs=pl.BlockSpec((1, cl, ML_WIDTH), lambda b_, c: (b_, c, 0)),
        scratch_shapes=[
            pltpu.VMEM((cl + SUBLANES, 2 * ML_QK_WIDTH), F32),
            pltpu.VMEM((ML_HEADS, ML_QK, ML_V), F32),
            pltpu.VMEM((ML_HEADS, 1, ML_QK), F32),
            pltpu.VMEM((ML_HEADS, 1, LANES), F32),
        ],
        compiler_params=_params("parallel", "arbitrary"),
        name="mlstm_cell",
    )(raw, v, gates, og, conv_w, conv_b, head_norm)


def _rope_table(positions):
    half = QK_ROPE // 2
    inv_freq = ROPE_THETA ** (-jnp.arange(0, QK_ROPE, 2, dtype=F32) / QK_ROPE)
    ang = positions.astype(F32)[..., None] * inv_freq
    cos, sin = jnp.cos(ang), jnp.sin(ang)
    z = jnp.zeros_like(cos)
    tab = jnp.concatenate([cos, cos, z, z, -sin, z, z, z, z, sin, z, z], axis=-1)
    return tab.reshape(-1, 12 * half)


def _mla_layer(x, xb, rope_tab, bsz, seq, w_in, q_norm, w_qb, kv_norm, w_kvb, w_out, lw, lb, last):
    o_kv = Q_LORA + KV_LORA
    o_z = o_kv + QK_ROPE
    w_c = w_in[:, :o_kv].astype(BF16)
    w_kr = jnp.pad(w_in[:, o_kv:o_z], ((0, 0), (0, LANES - QK_ROPE))).astype(BF16)
    w_z = w_in[:, o_z:].astype(BF16)
    norms = jnp.concatenate([q_norm, kv_norm]).reshape(1, o_kv)
    w_q = jnp.pad(w_qb.reshape(Q_LORA, MLA_HEADS, MLA_QK),
                  ((0, 0), (0, 0), (0, QK_PAD - MLA_QK))).reshape(Q_LORA, MLA_HEADS * QK_PAD).astype(BF16)
    w_kv3 = w_kvb.reshape(KV_LORA, MLA_HEADS, QK_NOPE + V_HEAD)
    w_k = w_kv3[:, :, :QK_NOPE].reshape(KV_LORA, MLA_HEADS * QK_NOPE).astype(BF16)
    w_vt = w_kv3[:, :, QK_NOPE:].reshape(KV_LORA, MLA_WIDTH).T.astype(BF16)
    tq, tk = min(ATTN_TQ, seq), min(ATTN_TK, seq // 2)

    tm_w, tn_w = _wide_tiles(xb)
    c_n, kr = _proj(xb, [w_c], _ep_rmsnorm, BF16, tm=tm_w, tn=Q_LORA, col_aux=[norms], row_aux=[rope_tab],
                    side=(w_kr, jnp.zeros((1, LANES), F32), _ep_krope, BF16), name="mla_latent")
    gz = _proj(xb, [w_z], _ep_silu, BF16, tm=tm_w, tn=tn_w, name="mla_gate")
    q = _proj(c_n, [w_q], _ep_q, BF16, tm=2048, tn=1024, xcol=0, xk=Q_LORA, row_aux=[rope_tab],
              name="mla_q")
    k = _proj(c_n, [w_k], _ep_k, BF16, tm=2048, tn=1024, out_tn=2048, xcol=1, xk=KV_LORA,
              row_aux=[kr], name="mla_k")

    shape3 = lambda t: t.reshape(bsz, seq, t.shape[-1])
    vt = _values_transposed(shape3(c_n), w_vt, tk=tk)
    a = _attention(shape3(q), shape3(k), vt, shape3(gz), tq=tq)
    return _out_ln(a.reshape(bsz * seq, MLA_WIDTH), x, w_out.astype(BF16),
                   lw.reshape(1, -1), lb.reshape(1, -1), tm=512, with_bf16=not last)


def _mlstm_layer(x, xb, bsz, seq, w_in, conv_w, conv_b, gate_b, head_norm, w_out, lw, lb, last):
    o1 = 2 * ML_QK_WIDTH
    o2 = o1 + ML_WIDTH
    o3 = o2 + ML_WIDTH
    o4 = o3 + 2 * ML_HEADS
    w_qk = w_in[:, :o1].astype(BF16)
    w_v = w_in[:, o1:o2].astype(BF16)
    w_o = w_in[:, o2:o3].astype(BF16)
    w_g = jnp.pad(w_in[:, o3:o4], ((0, 0), (0, GATE_PAD - 2 * ML_HEADS))).astype(BF16)
    w_z = w_in[:, o4:].astype(BF16)
    gb = jnp.pad(gate_b, (0, GATE_PAD - 2 * ML_HEADS)).reshape(1, GATE_PAD)

    tm_w, tn_w = _wide_tiles(xb)
    raw = _proj(xb, [w_qk], _ep_identity, F32, tm=1024, tn=tn_w, name="ml_qk")
    v, gates = _proj(xb, [w_v], _ep_identity, BF16, tm=tm_w, tn=tn_w,
                     side=(w_g, gb, _ep_bias, F32), name="ml_v")
    og = _proj(xb, [w_o, w_z], _ep_outgate, BF16, tm=1024, tn=1024, name="ml_outgate")

    shape3 = lambda t: t.reshape(bsz, seq, t.shape[-1])
    cell = _mlstm_cell(shape3(raw), shape3(v), shape3(gates), shape3(og), conv_w,
                       conv_b.reshape(1, -1), head_norm.reshape(1, -1), cl=256)
    return _out_ln(cell.reshape(bsz * seq, ML_WIDTH), x, w_out.astype(BF16),
                   lw.reshape(1, -1), lb.reshape(1, -1), tm=512, with_bf16=not last)


def kernel(x, positions, mla_w_in, mla_q_norm, mla_w_qb, mla_kv_norm, mla_w_kvb, mla_w_out,
           ml_w_in, ml_conv_w, ml_conv_b, ml_gate_b, ml_head_norm, ml_w_out, ln_w, ln_b):
    bsz, seq, _ = x.shape
    rope_tab = _rope_table(positions)
    h = x.reshape(bsz * seq, D_MODEL)
    hb = h
    for layer in range(DEPTH):
        j = layer // N_MIXERS
        last = layer == DEPTH - 1
        if layer % N_MIXERS == 0:
            outs = _mla_layer(h, hb, rope_tab, bsz, seq, mla_w_in[j], mla_q_norm[j], mla_w_qb[j],
                              mla_kv_norm[j], mla_w_kvb[j], mla_w_out[j], ln_w[layer], ln_b[layer], last)
        else:
            outs = _mlstm_layer(h, hb, bsz, seq, ml_w_in[j], ml_conv_w[j], ml_conv_b[j], ml_gate_b[j],
                                ml_head_norm[j], ml_w_out[j], ln_w[layer], ln_b[layer], last)
        h, hb = outs[0], outs[-1]
    return h.reshape(bsz, seq, D_MODEL)
```

```python
import functools

import jax
import jax.numpy as jnp
from jax import lax
from jax.experimental import pallas as pl
from jax.experimental.pallas import tpu as pltpu

D_MODEL = 2048
DEPTH = 4
CHUNK = 64
N_MIXERS = 2

MLA_HEADS = 16
QK_NOPE = 128
QK_ROPE = 64
V_HEAD = 128
Q_LORA = 512
KV_LORA = 512
ROPE_THETA = 10000.0
MLA_QK = QK_NOPE + QK_ROPE
MLA_WIDTH = MLA_HEADS * V_HEAD
QK_PAD = 256
ML_HEADS = 8
ML_QK = 128
ML_V = 256
ML_QK_WIDTH = ML_HEADS * ML_QK
ML_WIDTH = ML_HEADS * ML_V
CONV_W = 4
GATE_PAD = 128

ALPHA = (2 * DEPTH) ** 0.25
EPS = 1e-6

LANES = 128
SUBLANES = 8
VMEM_LIMIT = 56 * 1024 * 1024

F32 = jnp.float32
BF16 = jnp.bfloat16
NEG_INF = float("-inf")
LOG2E = 1.4426950408889634

ATTN_TQ = 2048
ATTN_TK = 1024
ATTN_STRIP = 256


def _params(*sem):
    return pltpu.CompilerParams(dimension_semantics=sem, vmem_limit_bytes=VMEM_LIMIT)


def _wide_tiles(x):
    return (2048, 2048) if x.dtype == BF16 else (1024, 1024)


def _silu(x):
    return x * jax.nn.sigmoid(x)


def _rope_lanes(v, tab):
    half = QK_ROPE // 2
    c = tab[:, 0:LANES]
    sa = tab[:, LANES:2 * LANES]
    sb = tab[:, 2 * LANES:3 * LANES]
    return v * c + pltpu.roll(v, LANES - half, 1) * sa + pltpu.roll(v, half, 1) * sb


def _proj_kernel(*refs, n_w, n_col, n_row, epilogue, cast_x, side_epilogue):
    x_ref = refs[0]
    w_refs = refs[1:1 + n_w]
    col_refs = refs[1 + n_w:1 + n_w + n_col]
    row_refs = refs[1 + n_w + n_col:1 + n_w + n_col + n_row]
    pos = 1 + n_w + n_col + n_row
    if side_epilogue is not None:
        sw_ref, sc_ref = refs[pos:pos + 2]
        pos += 2
    o_ref = refs[pos]
    first = pl.program_id(1) == 0
    if cast_x:
        xb_ref = refs[-1]

        @pl.when(first)
        def _():
            xb_ref[...] = x_ref[...].astype(BF16)

        xb = xb_ref[...]
    else:
        xb = x_ref[...]
    rows = [r[...] for r in row_refs]
    accs = [jnp.dot(xb, w[...], preferred_element_type=F32) for w in w_refs]
    o_ref[...] = epilogue(accs, [c[...] for c in col_refs], rows).astype(o_ref.dtype)
    if side_epilogue is not None:
        so_ref = refs[pos + 1]

        @pl.when(first)
        def _():
            acc = jnp.dot(xb, sw_ref[...], preferred_element_type=F32)
            so_ref[...] = side_epilogue([acc], [sc_ref[...]], rows).astype(so_ref.dtype)


def _proj(x, ws, epilogue, out_dtype, *, tm, tn, out_tn=None, xcol=0, xk=None,
          col_aux=(), row_aux=(), side=None, name):
    m = x.shape[0]
    xk = x.shape[1] if xk is None else xk
    n = ws[0].shape[1]
    tm = min(tm, m)
    tn = min(tn, n)
    out_tn = tn if out_tn is None else out_tn
    nj = n // tn
    cast_x = x.dtype != BF16
    in_specs = [pl.BlockSpec((tm, xk), lambda i, j: (i, xcol))]
    in_specs += [pl.BlockSpec((xk, tn), lambda i, j: (0, j)) for _ in ws]
    in_specs += [pl.BlockSpec((1, tn), lambda i, j: (0, j)) for _ in col_aux]
    in_specs += [pl.BlockSpec((tm, r.shape[1]), lambda i, j: (i, 0)) for r in row_aux]
    out_shape = jax.ShapeDtypeStruct((m, nj * out_tn), out_dtype)
    out_specs = pl.BlockSpec((tm, out_tn), lambda i, j: (i, j))
    side_args = ()
    if side is not None:
        w_side, col_side, side_epilogue, side_dtype = side
        ns = w_side.shape[1]
        in_specs += [pl.BlockSpec((xk, ns), lambda i, j: (0, 0)), pl.BlockSpec((1, ns), lambda i, j: (0, 0))]
        out_shape = (out_shape, jax.ShapeDtypeStruct((m, ns), side_dtype))
        out_specs = (out_specs, pl.BlockSpec((tm, ns), lambda i, j: (i, 0)))
        side_args = (w_side, col_side)
    kern = functools.partial(_proj_kernel, n_w=len(ws), n_col=len(col_aux), n_row=len(row_aux),
                             epilogue=epilogue, cast_x=cast_x,
                             side_epilogue=None if side is None else side[2])
    return pl.pallas_call(
        kern,
        out_shape=out_shape,
        grid=(m // tm, nj),
        in_specs=in_specs,
        out_specs=out_specs,
        scratch_shapes=[pltpu.VMEM((tm, xk), BF16)] if cast_x else [],
        compiler_params=_params("parallel", "arbitrary"),
        name=name,
    )(x, *ws, *col_aux, *row_aux, *side_args)


def _ep_identity(accs, cols, rows):
    return accs[0]


def _ep_silu(accs, cols, rows):
    return _silu(accs[0])


def _ep_rmsnorm(accs, cols, rows):
    a = accs[0]
    return a * lax.rsqrt(jnp.mean(a * a, axis=-1, keepdims=True) + EPS) * cols[0]


def _ep_krope(accs, cols, rows):
    return _rope_lanes(accs[0], rows[0])


def _ep_q(accs, cols, rows):
    a = accs[0]
    scale = MLA_QK ** -0.5 * LOG2E
    parts = []
    for h in range(a.shape[1] // QK_PAD):
        nope = a[:, h * QK_PAD:h * QK_PAD + QK_NOPE]
        rp = a[:, h * QK_PAD + QK_NOPE:(h + 1) * QK_PAD]
        parts += [nope * scale, _rope_lanes(rp, rows[0]) * scale]
    return jnp.concatenate(parts, axis=1)


def _ep_k(accs, cols, rows):
    a = accs[0]
    kr = rows[0].astype(F32)
    parts = []
    for h in range(a.shape[1] // QK_NOPE):
        parts += [a[:, h * QK_NOPE:(h + 1) * QK_NOPE], kr]
    return jnp.concatenate(parts, axis=1)


def _ep_outgate(accs, cols, rows):
    return jax.nn.sigmoid(accs[0]) * _silu(accs[1])


def _ep_bias(accs, cols, rows):
    return accs[0] + cols[0]


def _attn_kernel(q_ref, k_ref, vt_ref, g_ref, o_ref, s0_ref, s1_ref, mb0_ref, mb1_ref, m_ref, l_ref, acc_ref,
                 *, tq, tk):
    i = pl.program_id(2)
    m_ref[...] = jnp.full(m_ref.shape, NEG_INF, F32)
    l_ref[...] = jnp.zeros(l_ref.shape, F32)
    acc_ref[...] = jnp.zeros(acc_ref.shape, F32)

    def visible_rows(c, c0):
        return min(tk, c - c0 + ATTN_STRIP)

    def score_strip(j, s_ref, mb_ref, c, c0=0, diag=False):
        rows = visible_rows(c, c0) if diag else tk
        cs = slice(c, c + ATTN_STRIP)
        s = lax.dot_general(k_ref[0, pl.ds(pl.multiple_of(j * tk, tk), rows), :], q_ref[0, cs, :],
                            (((1,), (1,)), ((), ())), preferred_element_type=F32)
        s_ref[0:rows, cs] = s
        if rows == tk:
            mb_ref[:, cs] = jnp.max(s, axis=0, keepdims=True)

    def update_strip(j, s_ref, mb_ref, c, c0=0, masked=False):
        cs = slice(c, c + ATTN_STRIP)
        rows = visible_rows(c, c0) if masked else tk
        s = s_ref[0:rows, cs]
        if masked and c < c0 + tk:
            r = lax.broadcasted_iota(jnp.int32, s.shape, 0) // CHUNK
            cc = (lax.broadcasted_iota(jnp.int32, s.shape, 1) + (c - c0)) // CHUNK
            s = jnp.where(r <= cc, s, NEG_INF)
            m_blk = jnp.max(s, axis=0, keepdims=True)
        else:
            m_blk = mb_ref[:, cs]
        m_prev = m_ref[:, cs]
        m_new = jnp.maximum(m_prev, m_blk)
        alpha = jnp.exp2(m_prev - m_new)
        p = jnp.exp2(s - m_new)
        l_ref[:, cs] = alpha * l_ref[:, cs] + jnp.sum(p, axis=0, keepdims=True)
        acc_ref[:, cs] = alpha * acc_ref[:, cs] + jnp.dot(vt_ref[0, j, :, 0:rows], p.astype(BF16),
                                                          preferred_element_type=F32)
        m_ref[:, cs] = m_new

    nsub = tq // tk
    bufs = ((s0_ref, mb0_ref), (s1_ref, mb1_ref))
    strips = list(range(0, tq, ATTN_STRIP))
    for c in strips:
        score_strip(0, *bufs[0], c)

    def body(jj, carry):
        for d in range(nsub):
            cur, nxt = bufs[d % 2], bufs[(d + 1) % 2]
            j = nsub * jj + d
            score_strip(j + 1, *nxt, strips[0])
            for n, c in enumerate(strips):
                if n + 1 < len(strips):
                    score_strip(j + 1, *nxt, strips[n + 1])
                update_strip(j, *cur, c)
        return carry

    lax.fori_loop(0, i, body, 0)
    for d in range(nsub):
        cur, nxt = bufs[d % 2], bufs[(d + 1) % 2]
        c0 = d * tk
        for c in range(c0, tq, ATTN_STRIP):
            if d + 1 < nsub and c >= c0 + tk:
                score_strip(nsub * i + d + 1, *nxt, c, c0 + tk, diag=True)
            update_strip(nsub * i + d, *cur, c, c0, masked=True)
    for c in strips:
        cs = slice(c, c + ATTN_STRIP)
        o = (acc_ref[:, cs] / l_ref[:, cs]).T
        o_ref[0, cs, :] = (o * g_ref[0, cs, :].astype(F32)).astype(o_ref.dtype)


def _attention(q, k, vt, gz, *, tq):
    b, s, _ = q.shape
    tk = vt.shape[-1]
    assert tq % (2 * tk) == 0 and s % tq == 0
    return pl.pallas_call(
        functools.partial(_attn_kernel, tq=tq, tk=tk),
        out_shape=jax.ShapeDtypeStruct((b, s, MLA_WIDTH), BF16),
        grid=(b, MLA_HEADS, s // tq),
        in_specs=[
            pl.BlockSpec((1, tq, QK_PAD), lambda b_, h, i: (b_, i, h)),
            pl.BlockSpec((1, s, QK_PAD), lambda b_, h, i: (b_, 0, h)),
            pl.BlockSpec((1, s // tk, V_HEAD, tk), lambda b_, h, i: (b_, 0, h, 0)),
            pl.BlockSpec((1, tq, V_HEAD), lambda b_, h, i: (b_, i, h)),
        ],
        out_specs=pl.BlockSpec((1, tq, V_HEAD), lambda b_, h, i: (b_, i, h)),
        scratch_shapes=[pltpu.VMEM((tk, tq), F32), pltpu.VMEM((tk, tq), F32),
                        pltpu.VMEM((1, tq), F32), pltpu.VMEM((1, tq), F32),
                        pltpu.VMEM((1, tq), F32), pltpu.VMEM((1, tq), F32),
                        pltpu.VMEM((V_HEAD, tq), F32)],
        compiler_params=_params("parallel", "parallel", "arbitrary"),
        name="mla_attention",
    )(q, k, vt, gz)


def _vt_kernel(w_ref, c_ref, o_ref):
    o_ref[0, 0] = lax.dot_general(w_ref[...], c_ref[0], (((1,), (1,)), ((), ())),
                                  preferred_element_type=F32).astype(o_ref.dtype)


def _values_transposed(c_n, w_vt, *, tk):
    b, s, _ = c_n.shape
    return pl.pallas_call(
        _vt_kernel,
        out_shape=jax.ShapeDtypeStruct((b, s // tk, MLA_WIDTH, tk), BF16),
        grid=(b, s // tk),
        in_specs=[
            pl.BlockSpec((MLA_WIDTH, KV_LORA), lambda b_, j: (0, 0)),
            pl.BlockSpec((1, tk, KV_LORA), lambda b_, j: (b_, j, 1)),
        ],
        out_specs=pl.BlockSpec((1, 1, MLA_WIDTH, tk), lambda b_, j: (b_, j, 0, 0)),
        compiler_params=_params("parallel", "parallel"),
        name="mla_vt",
    )(w_vt, c_n)


def _out_ln_kernel(a_ref, x_ref, w_ref, lw_ref, lb_ref, o_ref, *maybe_ob_ref):
    y = jnp.dot(a_ref[...], w_ref[...], preferred_element_type=F32)
    t = ALPHA * x_ref[...] + y
    mu = jnp.mean(t, axis=-1, keepdims=True)
    d = t - mu
    var = jnp.mean(d * d, axis=-1, keepdims=True)
    out = d * lax.rsqrt(var + EPS) * lw_ref[...] + lb_ref[...]
    o_ref[...] = out
    for ob_ref in maybe_ob_ref:
        ob_ref[...] = out.astype(BF16)


def _out_ln(a, x, w, lw, lb, *, tm, with_bf16):
    m = x.shape[0]
    tm = min(tm, m)
    row_spec = pl.BlockSpec((tm, D_MODEL), lambda i: (i, 0))
    out_shape = [jax.ShapeDtypeStruct((m, D_MODEL), F32)]
    if with_bf16:
        out_shape.append(jax.ShapeDtypeStruct((m, D_MODEL), BF16))
    return pl.pallas_call(
        _out_ln_kernel,
        out_shape=out_shape,
        grid=(m // tm,),
        in_specs=[
            pl.BlockSpec((tm, a.shape[1]), lambda i: (i, 0)),
            row_spec,
            pl.BlockSpec(w.shape, lambda i: (0, 0)),
            pl.BlockSpec((1, D_MODEL), lambda i: (0, 0)),
            pl.BlockSpec((1, D_MODEL), lambda i: (0, 0)),
        ],
        out_specs=[row_spec] * len(out_shape),
        compiler_params=_params("parallel"),
        name="out_proj_layernorm",
    )(a, x, w, lw, lb)


def _log_sigmoid(x):
    return jnp.minimum(x, 0.0) - jnp.log1p(jnp.exp(-jnp.abs(x)))


def _cell_kernel(raw_ref, v_ref, g_ref, og_ref, cw_ref, cb_ref, hn_ref, o_ref,
                 xpad_ref, c_ref, m_ref, *, cl):
    @pl.when(pl.program_id(1) == 0)
    def _():
        xpad_ref[0:SUBLANES, :] = jnp.zeros((SUBLANES, xpad_ref.shape[1]), F32)
        c_ref[...] = jnp.zeros(c_ref.shape, F32)
        m_ref[...] = jnp.zeros(m_ref.shape, F32)

    xpad_ref[SUBLANES:SUBLANES + cl, :] = raw_ref[0]

    g = g_ref[0]
    rows = lax.broadcasted_iota(jnp.int32, (cl, cl), 0)
    cols = lax.broadcasted_iota(jnp.int32, (cl, cl), 1)
    tril = cols <= rows
    bcum = jnp.dot(tril.astype(F32), _log_sigmoid(g), preferred_element_type=F32,
                   precision=lax.Precision.HIGHEST)
    g_t = g.T
    bcum_t = bcum.T

    for h in range(ML_HEADS):
        def conv_silu(c0):
            acc = cb_ref[:, c0:c0 + ML_QK]
            for tap in range(CONV_W):
                off = SUBLANES - (CONV_W - 1) + tap
                acc = acc + xpad_ref[off:off + cl, c0:c0 + ML_QK] * cw_ref[tap:tap + 1, c0:c0 + ML_QK]
            return _silu(acc)

        q = conv_silu(h * ML_QK)
        k = conv_silu(ML_QK_WIDTH + h * ML_QK) * (ML_QK ** -0.5)
        v = v_ref[0, :, h * ML_V:(h + 1) * ML_V]
        qb = q.astype(BF16)

        li_row = g_t[h:h + 1, :]
        li_col = g[:, h:h + 1]
        b_row = bcum_t[ML_HEADS + h:ML_HEADS + h + 1, :]
        b_col = bcum[:, ML_HEADS + h:ML_HEADS + h + 1]
        m_st = m_ref[h][:, 0:1]
        c_st = c_ref[h]
        v_ext = jnp.concatenate([v, jnp.ones((cl, LANES), BF16)], axis=1)

        dmat = jnp.where(tril, b_col + (li_row - b_row), NEG_INF)
        m_inter = b_col + m_st
        m_t = jnp.maximum(m_inter, jnp.max(dmat, axis=-1, keepdims=True))
        w_inter = jnp.exp(m_inter - m_t)
        s_qk = lax.dot_general(qb, k.astype(BF16), (((1,), (1,)), ((), ())),
                               preferred_element_type=F32) * jnp.exp(dmat - m_t)
        num = (w_inter * jnp.dot(qb, c_st.astype(BF16), preferred_element_type=F32)
               + jnp.dot(s_qk.astype(BF16), v_ext, preferred_element_type=F32))
        den = num[:, ML_V:]
        inv = 1.0 / jnp.maximum(jnp.abs(den), jnp.exp(-m_t))
        h_out = num[:, :ML_V] * jnp.concatenate([inv, inv], axis=1)

        b_last = b_col[cl - 1:cl, :]
        g_col = b_last - b_col + li_col
        m_new = jnp.maximum(b_last + m_st, jnp.max(g_col, axis=0, keepdims=True))
        decay = jnp.exp(b_last + m_st - m_new)
        kw = k * jnp.exp(g_col - m_new)
        c_ref[h] = decay * c_st + jnp.dot(kw.T.astype(BF16), v_ext, preferred_element_type=F32)
        m_ref[h] = jnp.broadcast_to(m_new, (1, LANES))

        hn = h_out * lax.rsqrt(jnp.mean(h_out * h_out, axis=-1, keepdims=True) + EPS)
        hn = hn * hn_ref[:, h * ML_V:(h + 1) * ML_V]
        gate = og_ref[0, :, h * ML_V:(h + 1) * ML_V].astype(F32)
        o_ref[0, :, h * ML_V:(h + 1) * ML_V] = (hn * gate).astype(o_ref.dtype)

    xpad_ref[0:SUBLANES, :] = xpad_ref[cl:cl + SUBLANES, :]


def _mlstm_cell(raw, v, gates, og, conv_w, conv_b, head_norm, *, cl):
    b, s, _ = raw.shape
    cl = min(cl, s)
    return pl.pallas_call(
        functools.partial(_cell_kernel, cl=cl),
        out_shape=jax.ShapeDtypeStruct((b, s, ML_WIDTH), BF16),
        grid=(b, s // cl),
        in_specs=[
            pl.BlockSpec((1, cl, 2 * ML_QK_WIDTH), lambda b_, c: (b_, c, 0)),
            pl.BlockSpec((1, cl, ML_WIDTH), lambda b_, c: (b_, c, 0)),
            pl.BlockSpec((1, cl, GATE_PAD), lambda b_, c: (b_, c, 0)),
            pl.BlockSpec((1, cl, ML_WIDTH), lambda b_, c: (b_, c, 0)),
            pl.BlockSpec((CONV_W, 2 * ML_QK_WIDTH), lambda b_, c: (0, 0)),
            pl.BlockSpec((1, 2 * ML_QK_WIDTH), lambda b_, c: (0, 0)),
            pl.BlockSpec((1, ML_WIDTH), lambda b_, c: (0, 0)),
        ],
        out_specs=pl.BlockSpec((1, cl, ML_WIDTH), lambda b_, c: (b_, c, 0)),
        scratch_shapes=[
            pltpu.VMEM((cl + SUBLANES, 2 * ML_QK_WIDTH), F32),
            pltpu.VMEM((ML_HEADS, ML_QK, ML_V + LANES), F32),
            pltpu.VMEM((ML_HEADS, 1, LANES), F32),
        ],
        compiler_params=_params("parallel", "arbitrary"),
        name="mlstm_cell",
    )(raw, v, gates, og, conv_w, conv_b, head_norm)


def _rope_table(positions):
    half = QK_ROPE // 2
    lane = jnp.arange(LANES)
    inv_freq = ROPE_THETA ** (-(2 * (lane % half)).astype(F32) / QK_ROPE)
    ang = positions.reshape(-1, 1).astype(F32) * inv_freq
    cos, sin = jnp.cos(ang), jnp.sin(ang)
    c = jnp.where(lane < 2 * half, cos, 0.0)
    sa = jnp.where(lane < half, -sin, 0.0)
    sb = jnp.where((lane >= half) & (lane < 2 * half), sin, 0.0)
    return jnp.concatenate([c, sa, sb], axis=-1)


def _mla_layer(x, xb, rope_tab, bsz, seq, w_in, q_norm, w_qb, kv_norm, w_kvb, w_out, lw, lb, last):
    o_kv = Q_LORA + KV_LORA
    o_z = o_kv + QK_ROPE
    w_c = w_in[:, :o_kv].astype(BF16)
    w_kr = jnp.pad(w_in[:, o_kv:o_z], ((0, 0), (0, LANES - QK_ROPE))).astype(BF16)
    w_z = w_in[:, o_z:].astype(BF16)
    norms = jnp.concatenate([q_norm, kv_norm]).reshape(1, o_kv)
    w_q = jnp.pad(w_qb.reshape(Q_LORA, MLA_HEADS, MLA_QK),
                  ((0, 0), (0, 0), (0, QK_PAD - MLA_QK))).reshape(Q_LORA, MLA_HEADS * QK_PAD).astype(BF16)
    w_kv3 = w_kvb.reshape(KV_LORA, MLA_HEADS, QK_NOPE + V_HEAD)
    w_k = w_kv3[:, :, :QK_NOPE].reshape(KV_LORA, MLA_HEADS * QK_NOPE).astype(BF16)
    w_vt = w_kv3[:, :, QK_NOPE:].reshape(KV_LORA, MLA_WIDTH).T.astype(BF16)
    tq, tk = min(ATTN_TQ, seq), min(ATTN_TK, seq // 2)

    tm_w, tn_w = _wide_tiles(xb)
    c_n, kr = _proj(xb, [w_c], _ep_rmsnorm, BF16, tm=tm_w, tn=Q_LORA, col_aux=[norms], row_aux=[rope_tab],
                    side=(w_kr, jnp.zeros((1, LANES), F32), _ep_krope, BF16), name="mla_latent")
    gz = _proj(xb, [w_z], _ep_silu, BF16, tm=tm_w, tn=tn_w, name="mla_gate")
    q = _proj(c_n, [w_q], _ep_q, BF16, tm=2048, tn=1024, xcol=0, xk=Q_LORA, row_aux=[rope_tab],
              name="mla_q")
    k = _proj(c_n, [w_k], _ep_k, BF16, tm=2048, tn=1024, out_tn=2048, xcol=1, xk=KV_LORA,
              row_aux=[kr], name="mla_k")

    shape3 = lambda t: t.reshape(bsz, seq, t.shape[-1])
    vt = _values_transposed(shape3(c_n), w_vt, tk=tk)
    a = _attention(shape3(q), shape3(k), vt, shape3(gz), tq=tq)
    return _out_ln(a.reshape(bsz * seq, MLA_WIDTH), x, w_out.astype(BF16),
                   lw.reshape(1, -1), lb.reshape(1, -1), tm=512, with_bf16=not last)


def _mlstm_layer(x, xb, bsz, seq, w_in, conv_w, conv_b, gate_b, head_norm, w_out, lw, lb, last):
    o1 = 2 * ML_QK_WIDTH
    o2 = o1 + ML_WIDTH
    o3 = o2 + ML_WIDTH
    o4 = o3 + 2 * ML_HEADS
    w_qk = w_in[:, :o1].astype(BF16)
    w_v = w_in[:, o1:o2].astype(BF16)
    w_o = w_in[:, o2:o3].astype(BF16)
    w_g = jnp.pad(w_in[:, o3:o4], ((0, 0), (0, GATE_PAD - 2 * ML_HEADS))).astype(BF16)
    w_z = w_in[:, o4:].astype(BF16)
    gb = jnp.pad(gate_b, (0, GATE_PAD - 2 * ML_HEADS)).reshape(1, GATE_PAD)

    tm_w, tn_w = _wide_tiles(xb)
    raw = _proj(xb, [w_qk], _ep_identity, F32, tm=1024, tn=tn_w, name="ml_qk")
    v, gates = _proj(xb, [w_v], _ep_identity, BF16, tm=tm_w, tn=tn_w,
                     side=(w_g, gb, _ep_bias, F32), name="ml_v")
    og = _proj(xb, [w_o, w_z], _ep_outgate, BF16, tm=1024, tn=1024, name="ml_outgate")

    shape3 = lambda t: t.reshape(bsz, seq, t.shape[-1])
    cell = _mlstm_cell(shape3(raw), shape3(v), shape3(gates), shape3(og), conv_w,
                       conv_b.reshape(1, -1), head_norm.reshape(1, -1), cl=256)
    return _out_ln(cell.reshape(bsz * seq, ML_WIDTH), x, w_out.astype(BF16),
                   lw.reshape(1, -1), lb.reshape(1, -1), tm=512, with_bf16=not last)


def kernel(x, positions, mla_w_in, mla_q_norm, mla_w_qb, mla_kv_norm, mla_w_kvb, mla_w_out,
           ml_w_in, ml_conv_w, ml_conv_b, ml_gate_b, ml_head_norm, ml_w_out, ln_w, ln_b):
    bsz, seq, _ = x.shape
    rope_tab = _rope_table(positions)
    h = x.reshape(bsz * seq, D_MODEL)
    hb = h
    for layer in range(DEPTH):
        j = layer // N_MIXERS
        last = layer == DEPTH - 1
        if layer % N_MIXERS == 0:
            outs = _mla_layer(h, hb, rope_tab, bsz, seq, mla_w_in[j], mla_q_norm[j], mla_w_qb[j],
                              mla_kv_norm[j], mla_w_kvb[j], mla_w_out[j], ln_w[layer], ln_b[layer], last)
        else:
            outs = _mlstm_layer(h, hb, bsz, seq, ml_w_in[j], ml_conv_w[j], ml_conv_b[j], ml_gate_b[j],
                                ml_head_norm[j], ml_w_out[j], ln_w[layer], ln_b[layer], last)
        h, hb = outs[0], outs[-1]
    return h.reshape(bsz, seq, D_MODEL)
```

```python
import functools

import jax
import jax.numpy as jnp
from jax import lax
from jax.experimental import pallas as pl
from jax.experimental.pallas import tpu as pltpu

D_MODEL = 2048
DEPTH = 4
CHUNK = 64
N_MIXERS = 2

MLA_HEADS = 16
QK_NOPE = 128
QK_ROPE = 64
V_HEAD = 128
Q_LORA = 512
KV_LORA = 512
ROPE_THETA = 10000.0
MLA_QK = QK_NOPE + QK_ROPE
MLA_WIDTH = MLA_HEADS * V_HEAD
QK_PAD = 256
ML_HEADS = 8
ML_QK = 128
ML_V = 256
ML_QK_WIDTH = ML_HEADS * ML_QK
ML_WIDTH = ML_HEADS * ML_V
CONV_W = 4
GATE_PAD = 128

ALPHA = (2 * DEPTH) ** 0.25
EPS = 1e-6

LANES = 128
SUBLANES = 8
VMEM_LIMIT = 56 * 1024 * 1024

F32 = jnp.float32
BF16 = jnp.bfloat16
NEG_INF = float("-inf")
LOG2E = 1.4426950408889634

ATTN_TQ = 2048
ATTN_TK = 1024
ATTN_STRIP = 256


def _params(*sem):
    return pltpu.CompilerParams(dimension_semantics=sem, vmem_limit_bytes=VMEM_LIMIT)


def _wide_tiles(x):
    return (2048, 2048) if x.dtype == BF16 else (1024, 1024)


def _silu(x):
    return x * jax.nn.sigmoid(x)


def _rope_lanes(v, tab):
    half = QK_ROPE // 2
    c = tab[:, 0:LANES]
    sa = tab[:, LANES:2 * LANES]
    sb = tab[:, 2 * LANES:3 * LANES]
    return v * c + pltpu.roll(v, LANES - half, 1) * sa + pltpu.roll(v, half, 1) * sb


def _proj_kernel(*refs, n_w, n_col, n_row, epilogue, cast_x, side_epilogue):
    x_ref = refs[0]
    w_refs = refs[1:1 + n_w]
    col_refs = refs[1 + n_w:1 + n_w + n_col]
    row_refs = refs[1 + n_w + n_col:1 + n_w + n_col + n_row]
    pos = 1 + n_w + n_col + n_row
    if side_epilogue is not None:
        sw_ref, sc_ref = refs[pos:pos + 2]
        pos += 2
    o_ref = refs[pos]
    first = pl.program_id(1) == 0
    if cast_x:
        xb_ref = refs[-1]

        @pl.when(first)
        def _():
            xb_ref[...] = x_ref[...].astype(BF16)

        xb = xb_ref[...]
    else:
        xb = x_ref[...]
    rows = [r[...] for r in row_refs]
    accs = [jnp.dot(xb, w[...], preferred_element_type=F32) for w in w_refs]
    o_ref[...] = epilogue(accs, [c[...] for c in col_refs], rows).astype(o_ref.dtype)
    if side_epilogue is not None:
        so_ref = refs[pos + 1]

        @pl.when(first)
        def _():
            acc = jnp.dot(xb, sw_ref[...], preferred_element_type=F32)
            so_ref[...] = side_epilogue([acc], [sc_ref[...]], rows).astype(so_ref.dtype)


def _proj(x, ws, epilogue, out_dtype, *, tm, tn, out_tn=None, xcol=0, xk=None,
          col_aux=(), row_aux=(), side=None, name):
    m = x.shape[0]
    xk = x.shape[1] if xk is None else xk
    n = ws[0].shape[1]
    tm = min(tm, m)
    tn = min(tn, n)
    out_tn = tn if out_tn is None else out_tn
    nj = n // tn
    cast_x = x.dtype != BF16
    in_specs = [pl.BlockSpec((tm, xk), lambda i, j: (i, xcol))]
    in_specs += [pl.BlockSpec((xk, tn), lambda i, j: (0, j)) for _ in ws]
    in_specs += [pl.BlockSpec((1, tn), lambda i, j: (0, j)) for _ in col_aux]
    in_specs += [pl.BlockSpec((tm, r.shape[1]), lambda i, j: (i, 0)) for r in row_aux]
    out_shape = jax.ShapeDtypeStruct((m, nj * out_tn), out_dtype)
    out_specs = pl.BlockSpec((tm, out_tn), lambda i, j: (i, j))
    side_args = ()
    if side is not None:
        w_side, col_side, side_epilogue, side_dtype = side
        ns = w_side.shape[1]
        in_specs += [pl.BlockSpec((xk, ns), lambda i, j: (0, 0)), pl.BlockSpec((1, ns), lambda i, j: (0, 0))]
        out_shape = (out_shape, jax.ShapeDtypeStruct((m, ns), side_dtype))
        out_specs = (out_specs, pl.BlockSpec((tm, ns), lambda i, j: (i, 0)))
        side_args = (w_side, col_side)
    kern = functools.partial(_proj_kernel, n_w=len(ws), n_col=len(col_aux), n_row=len(row_aux),
                             epilogue=epilogue, cast_x=cast_x,
                             side_epilogue=None if side is None else side[2])
    return pl.pallas_call(
        kern,
        out_shape=out_shape,
        grid=(m // tm, nj),
        in_specs=in_specs,
        out_specs=out_specs,
        scratch_shapes=[pltpu.VMEM((tm, xk), BF16)] if cast_x else [],
        compiler_params=_params("parallel", "arbitrary"),
        name=name,
    )(x, *ws, *col_aux, *row_aux, *side_args)


def _ep_identity(accs, cols, rows):
    return accs[0]


def _ep_silu(accs, cols, rows):
    return _silu(accs[0])


def _ep_rmsnorm(accs, cols, rows):
    a = accs[0]
    return a * lax.rsqrt(jnp.mean(a * a, axis=-1, keepdims=True) + EPS) * cols[0]


def _ep_krope(accs, cols, rows):
    return _rope_lanes(accs[0], rows[0])


def _ep_q(accs, cols, rows):
    a = accs[0]
    scale = MLA_QK ** -0.5 * LOG2E
    parts = []
    for h in range(a.shape[1] // QK_PAD):
        nope = a[:, h * QK_PAD:h * QK_PAD + QK_NOPE]
        rp = a[:, h * QK_PAD + QK_NOPE:(h + 1) * QK_PAD]
        parts += [nope * scale, _rope_lanes(rp, rows[0]) * scale]
    return jnp.concatenate(parts, axis=1)


def _ep_k(accs, cols, rows):
    a = accs[0]
    kr = rows[0].astype(F32)
    parts = []
    for h in range(a.shape[1] // QK_NOPE):
        parts += [a[:, h * QK_NOPE:(h + 1) * QK_NOPE], kr]
    return jnp.concatenate(parts, axis=1)


def _ep_outgate(accs, cols, rows):
    return jax.nn.sigmoid(accs[0]) * _silu(accs[1])


def _ep_bias(accs, cols, rows):
    return accs[0] + cols[0]


def _attn_kernel(q_ref, k_ref, vt_ref, g_ref, o_ref, s0_ref, s1_ref, mb0_ref, mb1_ref, m_ref, l_ref, acc_ref,
                 *, tq, tk):
    i = pl.program_id(2)
    m_ref[...] = jnp.full(m_ref.shape, NEG_INF, F32)
    l_ref[...] = jnp.zeros(l_ref.shape, F32)
    acc_ref[...] = jnp.zeros(acc_ref.shape, F32)

    def visible_rows(c, c0):
        return min(tk, c - c0 + ATTN_STRIP)

    def score_strip(j, s_ref, mb_ref, c, c0=0, diag=False):
        rows = visible_rows(c, c0) if diag else tk
        cs = slice(c, c + ATTN_STRIP)
        s = lax.dot_general(k_ref[0, pl.ds(pl.multiple_of(j * tk, tk), rows), :], q_ref[0, cs, :],
                            (((1,), (1,)), ((), ())), preferred_element_type=F32)
        s_ref[0:rows, cs] = s
        if rows == tk:
            mb_ref[:, cs] = jnp.max(s, axis=0, keepdims=True)

    def update_strip(j, s_ref, mb_ref, c, c0=0, masked=False):
        cs = slice(c, c + ATTN_STRIP)
        rows = visible_rows(c, c0) if masked else tk
        s = s_ref[0:rows, cs]
        if masked and c < c0 + tk:
            r = lax.broadcasted_iota(jnp.int32, s.shape, 0) // CHUNK
            cc = (lax.broadcasted_iota(jnp.int32, s.shape, 1) + (c - c0)) // CHUNK
            s = jnp.where(r <= cc, s, NEG_INF)
            m_blk = jnp.max(s, axis=0, keepdims=True)
        else:
            m_blk = mb_ref[:, cs]
        m_prev = m_ref[:, cs]
        m_new = jnp.maximum(m_prev, m_blk)
        alpha = jnp.exp2(m_prev - m_new)
        p = jnp.exp2(s - m_new)
        l_ref[:, cs] = alpha * l_ref[:, cs] + jnp.sum(p, axis=0, keepdims=True)
        acc_ref[:, cs] = alpha * acc_ref[:, cs] + jnp.dot(vt_ref[0, j, :, 0:rows], p.astype(BF16),
                                                          preferred_element_type=F32)
        m_ref[:, cs] = m_new

    nsub = tq // tk
    bufs = ((s0_ref, mb0_ref), (s1_ref, mb1_ref))
    strips = list(range(0, tq, ATTN_STRIP))
    for c in strips:
        score_strip(0, *bufs[0], c)

    def body(jj, carry):
        for d in range(nsub):
            cur, nxt = bufs[d % 2], bufs[(d + 1) % 2]
            j = nsub * jj + d
            score_strip(j + 1, *nxt, strips[0])
            for n, c in enumerate(strips):
                if n + 1 < len(strips):
                    score_strip(j + 1, *nxt, strips[n + 1])
                update_strip(j, *cur, c)
        return carry

    lax.fori_loop(0, i, body, 0)
    for d in range(nsub):
        cur, nxt = bufs[d % 2], bufs[(d + 1) % 2]
        c0 = d * tk
        for c in range(c0, tq, ATTN_STRIP):
            if d + 1 < nsub and c >= c0 + tk:
                score_strip(nsub * i + d + 1, *nxt, c, c0 + tk, diag=True)
            update_strip(nsub * i + d, *cur, c, c0, masked=True)
    for c in strips:
        cs = slice(c, c + ATTN_STRIP)
        o = (acc_ref[:, cs] / l_ref[:, cs]).T
        o_ref[0, cs, :] = (o * g_ref[0, cs, :].astype(F32)).astype(o_ref.dtype)


def _attention(q, k, vt, gz, *, tq):
    b, s, _ = q.shape
    tk = vt.shape[-1]
    assert tq % (2 * tk) == 0 and s % tq == 0
    return pl.pallas_call(
        functools.partial(_attn_kernel, tq=tq, tk=tk),
        out_shape=jax.ShapeDtypeStruct((b, s, MLA_WIDTH), BF16),
        grid=(b, MLA_HEADS, s // tq),
        in_specs=[
            pl.BlockSpec((1, tq, QK_PAD), lambda b_, h, i: (b_, i, h)),
            pl.BlockSpec((1, s, QK_PAD), lambda b_, h, i: (b_, 0, h)),
            pl.BlockSpec((1, s // tk, V_HEAD, tk), lambda b_, h, i: (b_, 0, h, 0)),
            pl.BlockSpec((1, tq, V_HEAD), lambda b_, h, i: (b_, i, h)),
        ],
        out_specs=pl.BlockSpec((1, tq, V_HEAD), lambda b_, h, i: (b_, i, h)),
        scratch_shapes=[pltpu.VMEM((tk, tq), F32), pltpu.VMEM((tk, tq), F32),
                        pltpu.VMEM((1, tq), F32), pltpu.VMEM((1, tq), F32),
                        pltpu.VMEM((1, tq), F32), pltpu.VMEM((1, tq), F32),
                        pltpu.VMEM((V_HEAD, tq), F32)],
        compiler_params=_params("parallel", "parallel", "arbitrary"),
        name="mla_attention",
    )(q, k, vt, gz)


def _vt_kernel(w_ref, c_ref, o_ref):
    o_ref[0, 0] = lax.dot_general(w_ref[...], c_ref[0], (((1,), (1,)), ((), ())),
                                  preferred_element_type=F32).astype(o_ref.dtype)


def _values_transposed(c_n, w_vt, *, tk):
    b, s, _ = c_n.shape
    return pl.pallas_call(
        _vt_kernel,
        out_shape=jax.ShapeDtypeStruct((b, s // tk, MLA_WIDTH, tk), BF16),
        grid=(b, s // tk),
        in_specs=[
            pl.BlockSpec((MLA_WIDTH, KV_LORA), lambda b_, j: (0, 0)),
            pl.BlockSpec((1, tk, KV_LORA), lambda b_, j: (b_, j, 1)),
        ],
        out_specs=pl.BlockSpec((1, 1, MLA_WIDTH, tk), lambda b_, j: (b_, j, 0, 0)),
        compiler_params=_params("parallel", "parallel"),
        name="mla_vt",
    )(w_vt, c_n)


def _out_ln_kernel(a_ref, x_ref, w_ref, lw_ref, lb_ref, o_ref, *maybe_ob_ref):
    y = jnp.dot(a_ref[...], w_ref[...], preferred_element_type=F32)
    t = ALPHA * x_ref[...] + y
    mu = jnp.mean(t, axis=-1, keepdims=True)
    d = t - mu
    var = jnp.mean(d * d, axis=-1, keepdims=True)
    out = d * lax.rsqrt(var + EPS) * lw_ref[...] + lb_ref[...]
    o_ref[...] = out
    for ob_ref in maybe_ob_ref:
        ob_ref[...] = out.astype(BF16)


def _out_ln(a, x, w, lw, lb, *, tm, with_bf16):
    m = x.shape[0]
    tm = min(tm, m)
    row_spec = pl.BlockSpec((tm, D_MODEL), lambda i: (i, 0))
    out_shape = [jax.ShapeDtypeStruct((m, D_MODEL), F32)]
    if with_bf16:
        out_shape.append(jax.ShapeDtypeStruct((m, D_MODEL), BF16))
    return pl.pallas_call(
        _out_ln_kernel,
        out_shape=out_shape,
        grid=(m // tm,),
        in_specs=[
            pl.BlockSpec((tm, a.shape[1]), lambda i: (i, 0)),
            row_spec,
            pl.BlockSpec(w.shape, lambda i: (0, 0)),
            pl.BlockSpec((1, D_MODEL), lambda i: (0, 0)),
            pl.BlockSpec((1, D_MODEL), lambda i: (0, 0)),
        ],
        out_specs=[row_spec] * len(out_shape),
        compiler_params=_params("parallel"),
        name="out_proj_layernorm",
    )(a, x, w, lw, lb)


def _log_sigmoid(x):
    return jnp.minimum(x, 0.0) - jnp.log1p(jnp.exp(-jnp.abs(x)))


def _cell_kernel(raw_ref, v_ref, g_ref, og_ref, cw_ref, cb_ref, hn_ref, o_ref,
                 xpad_ref, c_ref, m_ref, *, cl):
    @pl.when(pl.program_id(1) == 0)
    def _():
        xpad_ref[0:SUBLANES, :] = jnp.zeros((SUBLANES, xpad_ref.shape[1]), F32)
        c_ref[...] = jnp.zeros(c_ref.shape, F32)
        m_ref[...] = jnp.zeros(m_ref.shape, F32)

    xpad_ref[SUBLANES:SUBLANES + cl, :] = raw_ref[0]

    g = g_ref[0]
    rows = lax.broadcasted_iota(jnp.int32, (cl, cl), 0)
    cols = lax.broadcasted_iota(jnp.int32, (cl, cl), 1)
    tril = cols <= rows
    bcum = jnp.dot(tril.astype(F32), _log_sigmoid(g), preferred_element_type=F32,
                   precision=lax.Precision.HIGHEST)
    g_t = g.T
    bcum_t = bcum.T
    b_t = bcum_t[ML_HEADS:2 * ML_HEADS, :]
    cmax_t = g_t[0:ML_HEADS, :] - b_t
    lane_t = lax.broadcasted_iota(jnp.int32, cmax_t.shape, 1)
    shift = 1
    while shift < cl:
        cmax_t = jnp.maximum(cmax_t, jnp.where(lane_t >= shift, pltpu.roll(cmax_t, shift, 1), NEG_INF))
        shift *= 2
    m_intra = jnp.concatenate([b_t + cmax_t, jnp.zeros((LANES - ML_HEADS, cl), F32)], axis=0).T

    for h in range(ML_HEADS):
        def conv_silu(c0):
            acc = cb_ref[:, c0:c0 + ML_QK]
            for tap in range(CONV_W):
                off = SUBLANES - (CONV_W - 1) + tap
                acc = acc + xpad_ref[off:off + cl, c0:c0 + ML_QK] * cw_ref[tap:tap + 1, c0:c0 + ML_QK]
            return _silu(acc)

        q = conv_silu(h * ML_QK)
        k = conv_silu(ML_QK_WIDTH + h * ML_QK) * (ML_QK ** -0.5)
        v = v_ref[0, :, h * ML_V:(h + 1) * ML_V]
        qb = q.astype(BF16)

        li_row = g_t[h:h + 1, :]
        li_col = g[:, h:h + 1]
        b_row = bcum_t[ML_HEADS + h:ML_HEADS + h + 1, :]
        b_col = bcum[:, ML_HEADS + h:ML_HEADS + h + 1]
        m_st = m_ref[h][:, 0:1]
        c_st = c_ref[h]
        v_ext = jnp.concatenate([v, jnp.ones((cl, LANES), BF16)], axis=1)

        dmat = jnp.where(tril, b_col + (li_row - b_row), NEG_INF)
        m_inter = b_col + m_st
        m_t = jnp.maximum(m_inter, m_intra[:, h:h + 1])
        w_inter = jnp.exp(m_inter - m_t)
        s_qk = lax.dot_general(qb, k.astype(BF16), (((1,), (1,)), ((), ())),
                               preferred_element_type=F32) * jnp.exp(dmat - m_t)
        num = (w_inter * jnp.dot(qb, c_st.astype(BF16), preferred_element_type=F32)
               + jnp.dot(s_qk.astype(BF16), v_ext, preferred_element_type=F32))
        den = num[:, ML_V:]
        inv = 1.0 / jnp.maximum(jnp.abs(den), jnp.exp(-m_t))
        h_out = num[:, :ML_V] * jnp.concatenate([inv, inv], axis=1)

        b_last = b_col[cl - 1:cl, :]
        g_col = b_last - b_col + li_col
        m_new = jnp.maximum(b_last + m_st, jnp.max(g_col, axis=0, keepdims=True))
        decay = jnp.exp(b_last + m_st - m_new)
        kw = k * jnp.exp(g_col - m_new)
        c_ref[h] = decay * c_st + jnp.dot(kw.T.astype(BF16), v_ext, preferred_element_type=F32)
        m_ref[h] = jnp.broadcast_to(m_new, (1, LANES))

        hn = h_out * lax.rsqrt(jnp.mean(h_out * h_out, axis=-1, keepdims=True) + EPS)
        hn = hn * hn_ref[:, h * ML_V:(h + 1) * ML_V]
        gate = og_ref[0, :, h * ML_V:(h + 1) * ML_V].astype(F32)
        o_ref[0, :, h * ML_V:(h + 1) * ML_V] = (hn * gate).astype(o_ref.dtype)

    xpad_ref[0:SUBLANES, :] = xpad_ref[cl:cl + SUBLANES, :]


def _mlstm_cell(raw, v, gates, og, conv_w, conv_b, head_norm, *, cl):
    b, s, _ = raw.shape
    cl = min(cl, s)
    return pl.pallas_call(
        functools.partial(_cell_kernel, cl=cl),
        out_shape=jax.ShapeDtypeStruct((b, s, ML_WIDTH), BF16),
        grid=(b, s // cl),
        in_specs=[
            pl.BlockSpec((1, cl, 2 * ML_QK_WIDTH), lambda b_, c: (b_, c, 0)),
            pl.BlockSpec((1, cl, ML_WIDTH), lambda b_, c: (b_, c, 0)),
            pl.BlockSpec((1, cl, GATE_PAD), lambda b_, c: (b_, c, 0)),
            pl.BlockSpec((1, cl, ML_WIDTH), lambda b_, c: (b_, c, 0)),
            pl.BlockSpec((CONV_W, 2 * ML_QK_WIDTH), lambda b_, c: (0, 0)),
            pl.BlockSpec((1, 2 * ML_QK_WIDTH), lambda b_, c: (0, 0)),
            pl.BlockSpec((1, ML_WIDTH), lambda b_, c: (0, 0)),
        ],
        out_specs=pl.BlockSpec((1, cl, ML_WIDTH), lambda b_, c: (b_, c, 0)),
        scratch_shapes=[
            pltpu.VMEM((cl + SUBLANES, 2 * ML_QK_WIDTH), F32),
            pltpu.VMEM((ML_HEADS, ML_QK, ML_V + LANES), F32),
            pltpu.VMEM((ML_HEADS, 1, LANES), F32),
        ],
        compiler_params=_params("parallel", "arbitrary"),
        name="mlstm_cell",
    )(raw, v, gates, og, conv_w, conv_b, head_norm)


def _rope_table(positions):
    half = QK_ROPE // 2
    lane = jnp.arange(LANES)
    inv_freq = ROPE_THETA ** (-(2 * (lane % half)).astype(F32) / QK_ROPE)
    ang = positions.reshape(-1, 1).astype(F32) * inv_freq
    cos, sin = jnp.cos(ang), jnp.sin(ang)
    c = jnp.where(lane < 2 * half, cos, 0.0)
    sa = jnp.where(lane < half, -sin, 0.0)
    sb = jnp.where((lane >= half) & (lane < 2 * half), sin, 0.0)
    return jnp.concatenate([c, sa, sb], axis=-1)


def _mla_layer(x, xb, rope_tab, bsz, seq, w_in, q_norm, w_qb, kv_norm, w_kvb, w_out, lw, lb, last):
    o_kv = Q_LORA + KV_LORA
    o_z = o_kv + QK_ROPE
    w_c = w_in[:, :o_kv].astype(BF16)
    w_kr = jnp.pad(w_in[:, o_kv:o_z], ((0, 0), (0, LANES - QK_ROPE))).astype(BF16)
    w_z = w_in[:, o_z:].astype(BF16)
    norms = jnp.concatenate([q_norm, kv_norm]).reshape(1, o_kv)
    w_q = jnp.pad(w_qb.reshape(Q_LORA, MLA_HEADS, MLA_QK),
                  ((0, 0), (0, 0), (0, QK_PAD - MLA_QK))).reshape(Q_LORA, MLA_HEADS * QK_PAD).astype(BF16)
    w_kv3 = w_kvb.reshape(KV_LORA, MLA_HEADS, QK_NOPE + V_HEAD)
    w_k = w_kv3[:, :, :QK_NOPE].reshape(KV_LORA, MLA_HEADS * QK_NOPE).astype(BF16)
    w_vt = w_kv3[:, :, QK_NOPE:].reshape(KV_LORA, MLA_WIDTH).T.astype(BF16)
    tq, tk = min(ATTN_TQ, seq), min(ATTN_TK, seq // 2)

    tm_w, tn_w = _wide_tiles(xb)
    c_n, kr = _proj(xb, [w_c], _ep_rmsnorm, BF16, tm=tm_w, tn=Q_LORA, col_aux=[norms], row_aux=[rope_tab],
                    side=(w_kr, jnp.zeros((1, LANES), F32), _ep_krope, BF16), name="mla_latent")
    gz = _proj(xb, [w_z], _ep_silu, BF16, tm=tm_w, tn=tn_w, name="mla_gate")
    q = _proj(c_n, [w_q], _ep_q, BF16, tm=2048, tn=1024, xcol=0, xk=Q_LORA, row_aux=[rope_tab],
              name="mla_q")
    k = _proj(c_n, [w_k], _ep_k, BF16, tm=2048, tn=1024, out_tn=2048, xcol=1, xk=KV_LORA,
              row_aux=[kr], name="mla_k")

    shape3 = lambda t: t.reshape(bsz, seq, t.shape[-1])
    vt = _values_transposed(shape3(c_n), w_vt, tk=tk)
    a = _attention(shape3(q), shape3(k), vt, shape3(gz), tq=tq)
    return _out_ln(a.reshape(bsz * seq, MLA_WIDTH), x, w_out.astype(BF16),
                   lw.reshape(1, -1), lb.reshape(1, -1), tm=512, with_bf16=not last)


def _mlstm_layer(x, xb, bsz, seq, w_in, conv_w, conv_b, gate_b, head_norm, w_out, lw, lb, last):
    o1 = 2 * ML_QK_WIDTH
    o2 = o1 + ML_WIDTH
    o3 = o2 + ML_WIDTH
    o4 = o3 + 2 * ML_HEADS
    w_qk = w_in[:, :o1].astype(BF16)
    w_v = w_in[:, o1:o2].astype(BF16)
    w_o = w_in[:, o2:o3].astype(BF16)
    w_g = jnp.pad(w_in[:, o3:o4], ((0, 0), (0, GATE_PAD - 2 * ML_HEADS))).astype(BF16)
    w_z = w_in[:, o4:].astype(BF16)
    gb = jnp.pad(gate_b, (0, GATE_PAD - 2 * ML_HEADS)).reshape(1, GATE_PAD)

    tm_w, tn_w = _wide_tiles(xb)
    raw = _proj(xb, [w_qk], _ep_identity, F32, tm=1024, tn=tn_w, name="ml_qk")
    v, gates = _proj(xb, [w_v], _ep_identity, BF16, tm=tm_w, tn=tn_w,
                     side=(w_g, gb, _ep_bias, F32), name="ml_v")
    og = _proj(xb, [w_o, w_z], _ep_outgate, BF16, tm=1024, tn=1024, name="ml_outgate")

    shape3 = lambda t: t.reshape(bsz, seq, t.shape[-1])
    cell = _mlstm_cell(shape3(raw), shape3(v), shape3(gates), shape3(og), conv_w,
                       conv_b.reshape(1, -1), head_norm.reshape(1, -1), cl=256)
    return _out_ln(cell.reshape(bsz * seq, ML_WIDTH), x, w_out.astype(BF16),
                   lw.reshape(1, -1), lb.reshape(1, -1), tm=512, with_bf16=not last)


def kernel(x, positions, mla_w_in, mla_q_norm, mla_w_qb, mla_kv_norm, mla_w_kvb, mla_w_out,
           ml_w_in, ml_conv_w, ml_conv_b, ml_gate_b, ml_head_norm, ml_w_out, ln_w, ln_b):
    bsz, seq, _ = x.shape
    rope_tab = _rope_table(positions)
    h = x.reshape(bsz * seq, D_MODEL)
    hb = h
    for layer in range(DEPTH):
        j = layer // N_MIXERS
        last = layer == DEPTH - 1
        if layer % N_MIXERS == 0:
            outs = _mla_layer(h, hb, rope_tab, bsz, seq, mla_w_in[j], mla_q_norm[j], mla_w_qb[j],
                              mla_kv_norm[j], mla_w_kvb[j], mla_w_out[j], ln_w[layer], ln_b[layer], last)
        else:
            outs = _mlstm_layer(h, hb, bsz, seq, ml_w_in[j], ml_conv_w[j], ml_conv_b[j], ml_gate_b[j],
                                ml_head_norm[j], ml_w_out[j], ln_w[layer], ln_b[layer], last)
        h, hb = outs[0], outs[-1]
    return h.reshape(bsz, seq, D_MODEL)
```

```python
import functools

import jax
import jax.numpy as jnp
from jax import lax
from jax.experimental import pallas as pl
from jax.experimental.pallas import tpu as pltpu

D_MODEL = 2048
DEPTH = 4
CHUNK = 64
N_MIXERS = 2

MLA_HEADS = 16
QK_NOPE = 128
QK_ROPE = 64
V_HEAD = 128
Q_LORA = 512
KV_LORA = 512
ROPE_THETA = 10000.0
MLA_QK = QK_NOPE + QK_ROPE
MLA_WIDTH = MLA_HEADS * V_HEAD
QK_PAD = 256
ML_HEADS = 8
ML_QK = 128
ML_V = 256
ML_QK_WIDTH = ML_HEADS * ML_QK
ML_WIDTH = ML_HEADS * ML_V
CONV_W = 4
GATE_PAD = 128

ALPHA = (2 * DEPTH) ** 0.25
EPS = 1e-6

LANES = 128
SUBLANES = 8
VMEM_LIMIT = 56 * 1024 * 1024

F32 = jnp.float32
BF16 = jnp.bfloat16
NEG_INF = float("-inf")
LOG2E = 1.4426950408889634

ATTN_TQ = 2048
ATTN_TK = 1024
ATTN_STRIP = 256


def _params(*sem):
    return pltpu.CompilerParams(dimension_semantics=sem, vmem_limit_bytes=VMEM_LIMIT)


def _wide_tiles(x):
    return (2048, 2048) if x.dtype == BF16 else (1024, 1024)


def _silu(x):
    return x * jax.nn.sigmoid(x)


def _rope_lanes(v, tab):
    half = QK_ROPE // 2
    c = tab[:, 0:LANES]
    sa = tab[:, LANES:2 * LANES]
    sb = tab[:, 2 * LANES:3 * LANES]
    return v * c + pltpu.roll(v, LANES - half, 1) * sa + pltpu.roll(v, half, 1) * sb


def _proj_kernel(*refs, n_w, n_col, n_row, epilogue, cast_x, side_epilogue):
    x_ref = refs[0]
    w_refs = refs[1:1 + n_w]
    col_refs = refs[1 + n_w:1 + n_w + n_col]
    row_refs = refs[1 + n_w + n_col:1 + n_w + n_col + n_row]
    pos = 1 + n_w + n_col + n_row
    if side_epilogue is not None:
        sw_ref, sc_ref = refs[pos:pos + 2]
        pos += 2
    o_ref = refs[pos]
    first = pl.program_id(1) == 0
    if cast_x:
        xb_ref = refs[-1]

        @pl.when(first)
        def _():
            xb_ref[...] = x_ref[...].astype(BF16)

        xb = xb_ref[...]
    else:
        xb = x_ref[...]
    rows = [r[...] for r in row_refs]
    accs = [jnp.dot(xb, w[...], preferred_element_type=F32) for w in w_refs]
    o_ref[...] = epilogue(accs, [c[...] for c in col_refs], rows).astype(o_ref.dtype)
    if side_epilogue is not None:
        so_ref = refs[pos + 1]

        @pl.when(first)
        def _():
            acc = jnp.dot(xb, sw_ref[...], preferred_element_type=F32)
            so_ref[...] = side_epilogue([acc], [sc_ref[...]], rows).astype(so_ref.dtype)


def _proj(x, ws, epilogue, out_dtype, *, tm, tn, out_tn=None, xcol=0, xk=None,
          col_aux=(), row_aux=(), side=None, name):
    m = x.shape[0]
    xk = x.shape[1] if xk is None else xk
    n = ws[0].shape[1]
    tm = min(tm, m)
    tn = min(tn, n)
    out_tn = tn if out_tn is None else out_tn
    nj = n // tn
    cast_x = x.dtype != BF16
    in_specs = [pl.BlockSpec((tm, xk), lambda i, j: (i, xcol))]
    in_specs += [pl.BlockSpec((xk, tn), lambda i, j: (0, j)) for _ in ws]
    in_specs += [pl.BlockSpec((1, tn), lambda i, j: (0, j)) for _ in col_aux]
    in_specs += [pl.BlockSpec((tm, r.shape[1]), lambda i, j: (i, 0)) for r in row_aux]
    out_shape = jax.ShapeDtypeStruct((m, nj * out_tn), out_dtype)
    out_specs = pl.BlockSpec((tm, out_tn), lambda i, j: (i, j))
    side_args = ()
    if side is not None:
        w_side, col_side, side_epilogue, side_dtype = side
        ns = w_side.shape[1]
        in_specs += [pl.BlockSpec((xk, ns), lambda i, j: (0, 0)), pl.BlockSpec((1, ns), lambda i, j: (0, 0))]
        out_shape = (out_shape, jax.ShapeDtypeStruct((m, ns), side_dtype))
        out_specs = (out_specs, pl.BlockSpec((tm, ns), lambda i, j: (i, 0)))
        side_args = (w_side, col_side)
    kern = functools.partial(_proj_kernel, n_w=len(ws), n_col=len(col_aux), n_row=len(row_aux),
                             epilogue=epilogue, cast_x=cast_x,
                             side_epilogue=None if side is None else side[2])
    return pl.pallas_call(
        kern,
        out_shape=out_shape,
        grid=(m // tm, nj),
        in_specs=in_specs,
        out_specs=out_specs,
        scratch_shapes=[pltpu.VMEM((tm, xk), BF16)] if cast_x else [],
        compiler_params=_params("parallel", "arbitrary"),
        name=name,
    )(x, *ws, *col_aux, *row_aux, *side_args)


def _ep_identity(accs, cols, rows):
    return accs[0]


def _ep_silu(accs, cols, rows):
    return _silu(accs[0])


def _ep_rmsnorm(accs, cols, rows):
    a = accs[0]
    return a * lax.rsqrt(jnp.mean(a * a, axis=-1, keepdims=True) + EPS) * cols[0]


def _ep_krope(accs, cols, rows):
    return _rope_lanes(accs[0], rows[0])


def _ep_q(accs, cols, rows):
    a = accs[0]
    scale = MLA_QK ** -0.5 * LOG2E
    parts = []
    for h in range(a.shape[1] // QK_PAD):
        nope = a[:, h * QK_PAD:h * QK_PAD + QK_NOPE]
        rp = a[:, h * QK_PAD + QK_NOPE:(h + 1) * QK_PAD]
        parts += [nope * scale, _rope_lanes(rp, rows[0]) * scale]
    return jnp.concatenate(parts, axis=1)


def _ep_k(accs, cols, rows):
    a = accs[0]
    kr = rows[0].astype(F32)
    parts = []
    for h in range(a.shape[1] // QK_NOPE):
        parts += [a[:, h * QK_NOPE:(h + 1) * QK_NOPE], kr]
    return jnp.concatenate(parts, axis=1)


def _ep_outgate(accs, cols, rows):
    return jax.nn.sigmoid(accs[0]) * _silu(accs[1])


def _ep_bias(accs, cols, rows):
    return accs[0] + cols[0]


def _attn_kernel(q_ref, k_ref, vt_ref, g_ref, o_ref, s0_ref, s1_ref, mb0_ref, mb1_ref, m_ref, l_ref, acc_ref,
                 *, tq, tk):
    i = pl.program_id(2)
    m_ref[...] = jnp.full(m_ref.shape, NEG_INF, F32)
    l_ref[...] = jnp.zeros(l_ref.shape, F32)
    acc_ref[...] = jnp.zeros(acc_ref.shape, F32)

    def visible_rows(c, c0):
        return min(tk, c - c0 + ATTN_STRIP)

    def score_strip(j, s_ref, mb_ref, c, c0=0, diag=False):
        rows = visible_rows(c, c0) if diag else tk
        cs = slice(c, c + ATTN_STRIP)
        s = lax.dot_general(k_ref[0, pl.ds(pl.multiple_of(j * tk, tk), rows), :], q_ref[0, cs, :],
                            (((1,), (1,)), ((), ())), preferred_element_type=F32)
        s_ref[0:rows, cs] = s
        if rows == tk:
            mb_ref[:, cs] = jnp.max(s, axis=0, keepdims=True)

    def update_strip(j, s_ref, mb_ref, c, c0=0, masked=False):
        cs = slice(c, c + ATTN_STRIP)
        rows = visible_rows(c, c0) if masked else tk
        s = s_ref[0:rows, cs]
        if masked and c < c0 + tk:
            r = lax.broadcasted_iota(jnp.int32, s.shape, 0) // CHUNK
            cc = (lax.broadcasted_iota(jnp.int32, s.shape, 1) + (c - c0)) // CHUNK
            s = jnp.where(r <= cc, s, NEG_INF)
            m_blk = jnp.max(s, axis=0, keepdims=True)
        else:
            m_blk = mb_ref[:, cs]
        m_prev = m_ref[:, cs]
        m_new = jnp.maximum(m_prev, m_blk)
        alpha = jnp.exp2(m_prev - m_new)
        p = jnp.exp2(s - m_new)
        l_ref[:, cs] = alpha * l_ref[:, cs] + jnp.sum(p, axis=0, keepdims=True)
        acc_ref[:, cs] = alpha * acc_ref[:, cs] + jnp.dot(vt_ref[0, j, :, 0:rows], p.astype(BF16),
                                                          preferred_element_type=F32)
        m_ref[:, cs] = m_new

    nsub = tq // tk
    bufs = ((s0_ref, mb0_ref), (s1_ref, mb1_ref))
    strips = list(range(0, tq, ATTN_STRIP))
    for c in strips:
        score_strip(0, *bufs[0], c)

    def body(jj, carry):
        for d in range(nsub):
            cur, nxt = bufs[d % 2], bufs[(d + 1) % 2]
            j = nsub * jj + d
            score_strip(j + 1, *nxt, strips[0])
            for n, c in enumerate(strips):
                if n + 1 < len(strips):
                    score_strip(j + 1, *nxt, strips[n + 1])
                update_strip(j, *cur, c)
        return carry

    lax.fori_loop(0, i, body, 0)
    for d in range(nsub):
        cur, nxt = bufs[d % 2], bufs[(d + 1) % 2]
        c0 = d * tk
        for c in range(c0, tq, ATTN_STRIP):
            if d + 1 < nsub and c >= c0 + tk:
                score_strip(nsub * i + d + 1, *nxt, c, c0 + tk, diag=True)
            update_strip(nsub * i + d, *cur, c, c0, masked=True)
    for c in strips:
        cs = slice(c, c + ATTN_STRIP)
        o = (acc_ref[:, cs] / l_ref[:, cs]).T
        o_ref[0, cs, :] = (o * g_ref[0, cs, :].astype(F32)).astype(o_ref.dtype)


def _attention(q, k, vt, gz, *, tq):
    b, s, _ = q.shape
    tk = vt.shape[-1]
    assert tq % (2 * tk) == 0 and s % tq == 0
    return pl.pallas_call(
        functools.partial(_attn_kernel, tq=tq, tk=tk),
        out_shape=jax.ShapeDtypeStruct((b, s, MLA_WIDTH), BF16),
        grid=(b, MLA_HEADS, s // tq),
        in_specs=[
            pl.BlockSpec((1, tq, QK_PAD), lambda b_, h, i: (b_, i, h)),
            pl.BlockSpec((1, s, QK_PAD), lambda b_, h, i: (b_, 0, h)),
            pl.BlockSpec((1, s // tk, V_HEAD, tk), lambda b_, h, i: (b_, 0, h, 0)),
            pl.BlockSpec((1, tq, V_HEAD), lambda b_, h, i: (b_, i, h)),
        ],
        out_specs=pl.BlockSpec((1, tq, V_HEAD), lambda b_, h, i: (b_, i, h)),
        scratch_shapes=[pltpu.VMEM((tk, tq), F32), pltpu.VMEM((tk, tq), F32),
                        pltpu.VMEM((1, tq), F32), pltpu.VMEM((1, tq), F32),
                        pltpu.VMEM((1, tq), F32), pltpu.VMEM((1, tq), F32),
                        pltpu.VMEM((V_HEAD, tq), F32)],
        compiler_params=_params("parallel", "parallel", "arbitrary"),
        name="mla_attention",
    )(q, k, vt, gz)


def _vt_kernel(w_ref, c_ref, o_ref):
    o_ref[0, 0] = lax.dot_general(w_ref[...], c_ref[0], (((1,), (1,)), ((), ())),
                                  preferred_element_type=F32).astype(o_ref.dtype)


def _values_transposed(c_n, w_vt, *, tk):
    b, s, _ = c_n.shape
    return pl.pallas_call(
        _vt_kernel,
        out_shape=jax.ShapeDtypeStruct((b, s // tk, MLA_WIDTH, tk), BF16),
        grid=(b, s // tk),
        in_specs=[
            pl.BlockSpec((MLA_WIDTH, KV_LORA), lambda b_, j: (0, 0)),
            pl.BlockSpec((1, tk, KV_LORA), lambda b_, j: (b_, j, 1)),
        ],
        out_specs=pl.BlockSpec((1, 1, MLA_WIDTH, tk), lambda b_, j: (b_, j, 0, 0)),
        compiler_params=_params("parallel", "parallel"),
        name="mla_vt",
    )(w_vt, c_n)


def _out_ln_kernel(a_ref, x_ref, w_ref, lw_ref, lb_ref, o_ref, *maybe_ob_ref):
    y = jnp.dot(a_ref[...], w_ref[...], preferred_element_type=F32)
    t = ALPHA * x_ref[...] + y
    mu = jnp.mean(t, axis=-1, keepdims=True)
    d = t - mu
    var = jnp.mean(d * d, axis=-1, keepdims=True)
    out = d * lax.rsqrt(var + EPS) * lw_ref[...] + lb_ref[...]
    o_ref[...] = out
    for ob_ref in maybe_ob_ref:
        ob_ref[...] = out.astype(BF16)


def _out_ln(a, x, w, lw, lb, *, tm, with_bf16):
    m = x.shape[0]
    tm = min(tm, m)
    row_spec = pl.BlockSpec((tm, D_MODEL), lambda i: (i, 0))
    out_shape = [jax.ShapeDtypeStruct((m, D_MODEL), F32)]
    if with_bf16:
        out_shape.append(jax.ShapeDtypeStruct((m, D_MODEL), BF16))
    return pl.pallas_call(
        _out_ln_kernel,
        out_shape=out_shape,
        grid=(m // tm,),
        in_specs=[
            pl.BlockSpec((tm, a.shape[1]), lambda i: (i, 0)),
            row_spec,
            pl.BlockSpec(w.shape, lambda i: (0, 0)),
            pl.BlockSpec((1, D_MODEL), lambda i: (0, 0)),
            pl.BlockSpec((1, D_MODEL), lambda i: (0, 0)),
        ],
        out_specs=[row_spec] * len(out_shape),
        compiler_params=_params("parallel"),
        name="out_proj_layernorm",
    )(a, x, w, lw, lb)


def _log_sigmoid(x):
    return jnp.minimum(x, 0.0) - jnp.log1p(jnp.exp(-jnp.abs(x)))


def _cell_kernel(raw_ref, v_ref, g_ref, og_ref, cw_ref, cb_ref, hn_ref, o_ref,
                 xpad_ref, c_ref, m_ref, *, cl):
    @pl.when(pl.program_id(1) == 0)
    def _():
        xpad_ref[0:SUBLANES, :] = jnp.zeros((SUBLANES, xpad_ref.shape[1]), F32)
        c_ref[...] = jnp.zeros(c_ref.shape, F32)
        m_ref[...] = jnp.zeros(m_ref.shape, F32)

    xpad_ref[SUBLANES:SUBLANES + cl, :] = raw_ref[0]

    g = g_ref[0]
    rows = lax.broadcasted_iota(jnp.int32, (cl, cl), 0)
    cols = lax.broadcasted_iota(jnp.int32, (cl, cl), 1)
    tril = cols <= rows
    bcum = jnp.dot(tril.astype(F32), _log_sigmoid(g), preferred_element_type=F32,
                   precision=lax.Precision.HIGHEST)
    g_t = g.T
    bcum_t = bcum.T
    b_t = bcum_t[ML_HEADS:2 * ML_HEADS, :]
    cmax_t = g_t[0:ML_HEADS, :] - b_t
    lane_t = lax.broadcasted_iota(jnp.int32, cmax_t.shape, 1)
    shift = 1
    while shift < cl:
        cmax_t = jnp.maximum(cmax_t, jnp.where(lane_t >= shift, pltpu.roll(cmax_t, shift, 1), NEG_INF))
        shift *= 2
    m_intra = jnp.concatenate([b_t + cmax_t, jnp.zeros((LANES - ML_HEADS, cl), F32)], axis=0).T

    for h in range(ML_HEADS):
        def conv_silu(c0):
            acc = cb_ref[:, c0:c0 + ML_QK]
            for tap in range(CONV_W):
                off = SUBLANES - (CONV_W - 1) + tap
                acc = acc + xpad_ref[off:off + cl, c0:c0 + ML_QK] * cw_ref[tap:tap + 1, c0:c0 + ML_QK]
            return _silu(acc)

        q = conv_silu(h * ML_QK)
        k = conv_silu(ML_QK_WIDTH + h * ML_QK) * (ML_QK ** -0.5)
        v = v_ref[0, :, h * ML_V:(h + 1) * ML_V]
        qb = q.astype(BF16)

        li_row = g_t[h:h + 1, :]
        li_col = g[:, h:h + 1]
        b_row = bcum_t[ML_HEADS + h:ML_HEADS + h + 1, :]
        b_col = bcum[:, ML_HEADS + h:ML_HEADS + h + 1]
        m_st = m_ref[h][:, 0:1]
        c_st = c_ref[h]
        v_ext = jnp.concatenate([v, jnp.ones((cl, LANES), BF16)], axis=1)

        dmat = jnp.where(tril, b_col + (li_row - b_row), NEG_INF)
        m_inter = b_col + m_st
        m_t = jnp.maximum(m_inter, m_intra[:, h:h + 1])
        w_inter = jnp.exp(m_inter - m_t)
        s_qk = lax.dot_general(qb, k.astype(BF16), (((1,), (1,)), ((), ())),
                               preferred_element_type=F32) * jnp.exp(dmat - m_t)
        num = (w_inter * jnp.dot(qb, c_st.astype(BF16), preferred_element_type=F32)
               + jnp.dot(s_qk.astype(BF16), v_ext, preferred_element_type=F32))
        den = num[:, ML_V:]
        inv = 1.0 / jnp.maximum(jnp.abs(den), jnp.exp(-m_t))
        h_out = num[:, :ML_V] * jnp.concatenate([inv, inv], axis=1)

        b_last = b_col[cl - 1:cl, :]
        g_col = b_last - b_col + li_col
        m_new = jnp.maximum(b_last + m_st, jnp.max(g_col, axis=0, keepdims=True))
        decay = jnp.exp(b_last + m_st - m_new)
        kw = k * jnp.exp(g_col - m_new)
        c_ref[h] = decay * c_st + jnp.dot(kw.T.astype(BF16), v_ext, preferred_element_type=F32)
        m_ref[h] = jnp.broadcast_to(m_new, (1, LANES))

        hn = h_out * lax.rsqrt(jnp.mean(h_out * h_out, axis=-1, keepdims=True) + EPS)
        hn = hn * hn_ref[:, h * ML_V:(h + 1) * ML_V]
        gate = og_ref[0, :, h * ML_V:(h + 1) * ML_V].astype(F32)
        o_ref[0, :, h * ML_V:(h + 1) * ML_V] = (hn * gate).astype(o_ref.dtype)

    xpad_ref[0:SUBLANES, :] = xpad_ref[cl:cl + SUBLANES, :]


ROWS_PER_STEP = 2


def _cell_rows_kernel(raw_ref, v_ref, g_ref, og_ref, cw_ref, cb_ref, hn_ref, o_ref, xpad_ref, c_ref, m_ref, *, cl):
    for r in range(raw_ref.shape[0]):
        one = pl.ds(r, 1)
        _cell_kernel(raw_ref.at[one], v_ref.at[one], g_ref.at[one], og_ref.at[one], cw_ref, cb_ref, hn_ref,
                     o_ref.at[one], xpad_ref.at[r], c_ref.at[r], m_ref.at[r], cl=cl)


def _mlstm_cell(raw, v, gates, og, conv_w, conv_b, head_norm, *, cl):
    b, s, _ = raw.shape
    cl = min(cl, s)
    nr = ROWS_PER_STEP if b % ROWS_PER_STEP == 0 else 1
    return pl.pallas_call(
        functools.partial(_cell_rows_kernel, cl=cl),
        out_shape=jax.ShapeDtypeStruct((b, s, ML_WIDTH), BF16),
        grid=(b // nr, s // cl),
        in_specs=[
            pl.BlockSpec((nr, cl, 2 * ML_QK_WIDTH), lambda b_, c: (b_, c, 0)),
            pl.BlockSpec((nr, cl, ML_WIDTH), lambda b_, c: (b_, c, 0)),
            pl.BlockSpec((nr, cl, GATE_PAD), lambda b_, c: (b_, c, 0)),
            pl.BlockSpec((nr, cl, ML_WIDTH), lambda b_, c: (b_, c, 0)),
            pl.BlockSpec((CONV_W, 2 * ML_QK_WIDTH), lambda b_, c: (0, 0)),
            pl.BlockSpec((1, 2 * ML_QK_WIDTH), lambda b_, c: (0, 0)),
            pl.BlockSpec((1, ML_WIDTH), lambda b_, c: (0, 0)),
        ],
        out_specs=pl.BlockSpec((nr, cl, ML_WIDTH), lambda b_, c: (b_, c, 0)),
        scratch_shapes=[
            pltpu.VMEM((nr, cl + SUBLANES, 2 * ML_QK_WIDTH), F32),
            pltpu.VMEM((nr, ML_HEADS, ML_QK, ML_V + LANES), F32),
            pltpu.VMEM((nr, ML_HEADS, 1, LANES), F32),
        ],
        compiler_params=_params("parallel", "arbitrary"),
        name="mlstm_cell",
    )(raw, v, gates, og, conv_w, conv_b, head_norm)


def _rope_table(positions):
    half = QK_ROPE // 2
    lane = jnp.arange(LANES)
    inv_freq = ROPE_THETA ** (-(2 * (lane % half)).astype(F32) / QK_ROPE)
    ang = positions.reshape(-1, 1).astype(F32) * inv_freq
    cos, sin = jnp.cos(ang), jnp.sin(ang)
    c = jnp.where(lane < 2 * half, cos, 0.0)
    sa = jnp.where(lane < half, -sin, 0.0)
    sb = jnp.where((lane >= half) & (lane < 2 * half), sin, 0.0)
    return jnp.concatenate([c, sa, sb], axis=-1)


def _mla_layer(x, xb, rope_tab, bsz, seq, w_in, q_norm, w_qb, kv_norm, w_kvb, w_out, lw, lb, last):
    o_kv = Q_LORA + KV_LORA
    o_z = o_kv + QK_ROPE
    w_c = w_in[:, :o_kv].astype(BF16)
    w_kr = jnp.pad(w_in[:, o_kv:o_z], ((0, 0), (0, LANES - QK_ROPE))).astype(BF16)
    w_z = w_in[:, o_z:].astype(BF16)
    norms = jnp.concatenate([q_norm, kv_norm]).reshape(1, o_kv)
    w_q = jnp.pad(w_qb.reshape(Q_LORA, MLA_HEADS, MLA_QK),
                  ((0, 0), (0, 0), (0, QK_PAD - MLA_QK))).reshape(Q_LORA, MLA_HEADS * QK_PAD).astype(BF16)
    w_kv3 = w_kvb.reshape(KV_LORA, MLA_HEADS, QK_NOPE + V_HEAD)
    w_k = w_kv3[:, :, :QK_NOPE].reshape(KV_LORA, MLA_HEADS * QK_NOPE).astype(BF16)
    w_vt = w_kv3[:, :, QK_NOPE:].reshape(KV_LORA, MLA_WIDTH).T.astype(BF16)
    tq, tk = min(ATTN_TQ, seq), min(ATTN_TK, seq // 2)

    tm_w, tn_w = _wide_tiles(xb)
    c_n, kr = _proj(xb, [w_c], _ep_rmsnorm, BF16, tm=tm_w, tn=Q_LORA, col_aux=[norms], row_aux=[rope_tab],
                    side=(w_kr, jnp.zeros((1, LANES), F32), _ep_krope, BF16), name="mla_latent")
    gz = _proj(xb, [w_z], _ep_silu, BF16, tm=tm_w, tn=tn_w, name="mla_gate")
    q = _proj(c_n, [w_q], _ep_q, BF16, tm=2048, tn=1024, xcol=0, xk=Q_LORA, row_aux=[rope_tab],
              name="mla_q")
    k = _proj(c_n, [w_k], _ep_k, BF16, tm=2048, tn=1024, out_tn=2048, xcol=1, xk=KV_LORA,
              row_aux=[kr], name="mla_k")

    shape3 = lambda t: t.reshape(bsz, seq, t.shape[-1])
    vt = _values_transposed(shape3(c_n), w_vt, tk=tk)
    a = _attention(shape3(q), shape3(k), vt, shape3(gz), tq=tq)
    return _out_ln(a.reshape(bsz * seq, MLA_WIDTH), x, w_out.astype(BF16),
                   lw.reshape(1, -1), lb.reshape(1, -1), tm=512, with_bf16=not last)


def _mlstm_layer(x, xb, bsz, seq, w_in, conv_w, conv_b, gate_b, head_norm, w_out, lw, lb, last):
    o1 = 2 * ML_QK_WIDTH
    o2 = o1 + ML_WIDTH
    o3 = o2 + ML_WIDTH
    o4 = o3 + 2 * ML_HEADS
    w_qk = w_in[:, :o1].astype(BF16)
    w_v = w_in[:, o1:o2].astype(BF16)
    w_o = w_in[:, o2:o3].astype(BF16)
    w_g = jnp.pad(w_in[:, o3:o4], ((0, 0), (0, GATE_PAD - 2 * ML_HEADS))).astype(BF16)
    w_z = w_in[:, o4:].astype(BF16)
    gb = jnp.pad(gate_b, (0, GATE_PAD - 2 * ML_HEADS)).reshape(1, GATE_PAD)

    tm_w, tn_w = _wide_tiles(xb)
    raw = _proj(xb, [w_qk], _ep_identity, F32, tm=1024, tn=tn_w, name="ml_qk")
    v, gates = _proj(xb, [w_v], _ep_identity, BF16, tm=tm_w, tn=tn_w,
                     side=(w_g, gb, _ep_bias, F32), name="ml_v")
    og = _proj(xb, [w_o, w_z], _ep_outgate, BF16, tm=1024, tn=1024, name="ml_outgate")

    shape3 = lambda t: t.reshape(bsz, seq, t.shape[-1])
    cell = _mlstm_cell(shape3(raw), shape3(v), shape3(gates), shape3(og), conv_w,
                       conv_b.reshape(1, -1), head_norm.reshape(1, -1), cl=256)
    return _out_ln(cell.reshape(bsz * seq, ML_WIDTH), x, w_out.astype(BF16),
                   lw.reshape(1, -1), lb.reshape(1, -1), tm=512, with_bf16=not last)


def kernel(x, positions, mla_w_in, mla_q_norm, mla_w_qb, mla_kv_norm, mla_w_kvb, mla_w_out,
           ml_w_in, ml_conv_w, ml_conv_b, ml_gate_b, ml_head_norm, ml_w_out, ln_w, ln_b):
    bsz, seq, _ = x.shape
    rope_tab = _rope_table(positions)
    h = x.reshape(bsz * seq, D_MODEL)
    hb = h
    for layer in range(DEPTH):
        j = layer // N_MIXERS
        last = layer == DEPTH - 1
        if layer % N_MIXERS == 0:
            outs = _mla_layer(h, hb, rope_tab, bsz, seq, mla_w_in[j], mla_q_norm[j], mla_w_qb[j],
                              mla_kv_norm[j], mla_w_kvb[j], mla_w_out[j], ln_w[layer], ln_b[layer], last)
        else:
            outs = _mlstm_layer(h, hb, bsz, seq, ml_w_in[j], ml_conv_w[j], ml_conv_b[j], ml_gate_b[j],
                                ml_head_norm[j], ml_w_out[j], ln_w[layer], ln_b[layer], last)
        h, hb = outs[0], outs[-1]
    return h.reshape(bsz, seq, D_MODEL)
```
